```python
import math
import jax, jax.numpy as jnp
from jax import lax
import numpy as np

D_MODEL = 2048
BATCH = 8
SEQ = 2048
DEPTH = 2

MIX_W = D_MODEL // 2
DIFF_HEADS = 8
DIFF_HEAD_DIM = MIX_W // DIFF_HEADS // 2
DIFF_V_DIM = 2 * DIFF_HEAD_DIM
CONV_CH = MIX_W
CONV_WIDTH = 31
RET_HEADS = 8
RET_HEAD_DIM = MIX_W // RET_HEADS
N_BRANCH = 3
XATTN_HEADS = 4
XATTN_HEAD_DIM = D_MODEL // XATTN_HEADS
MEM_TOKENS = 256
FFN_HIDDEN = -(-8 * D_MODEL // (3 * 256)) * 256
ROPE_THETA = 10000.0
Q_BLOCK = 128
RET_CHUNK = 128
EPS = 1e-6
NEG_INF = -1e30
IN_SPLITS = [MIX_W, MIX_W, MIX_W,
             2 * CONV_CH,
             MIX_W, MIX_W, MIX_W, MIX_W,
             N_BRANCH * D_MODEL]
IN_COLS = sum(IN_SPLITS)

kernel_name = "hybrid_diffattn_conformer_retention_block"


def rmsnorm(x, g):
    xf = x.astype(jnp.float32)
    y = xf * lax.rsqrt(jnp.mean(xf * xf, axis=-1, keepdims=True) + EPS)
    return (y * g.astype(jnp.float32)).astype(x.dtype)


def layernorm_f32(xf, g, b):
    mu = jnp.mean(xf, axis=-1, keepdims=True)
    var = jnp.mean(jnp.square(xf - mu), axis=-1, keepdims=True)
    return (xf - mu) * lax.rsqrt(var + EPS) * g.astype(jnp.float32) + b.astype(jnp.float32)


def rope_tables(T, dim, inv_freq):
    ang = jnp.arange(T, dtype=jnp.float32)[:, None] * inv_freq[None, :]
    return jnp.cos(ang), jnp.sin(ang)


def apply_rope(x, cos, sin):
    shape = (cos.shape[0],) + (1,) * (x.ndim - 3) + (cos.shape[1],)
    c, s = cos.reshape(shape), sin.reshape(shape)
    xf = x.astype(jnp.float32)
    x1, x2 = jnp.split(xf, 2, axis=-1)
    return jnp.concatenate([x1 * c - x2 * s, x2 * c + x1 * s], axis=-1)


def diff_attention(q, k, v, lam):
    B, T, H = q.shape[:3]
    n_blocks = T // Q_BLOCK
    scale = DIFF_HEAD_DIM ** -0.5
    vf = v.astype(jnp.float32)
    k_pos = jnp.arange(T)

    def one_block(blk):
        start = blk * Q_BLOCK
        qb = lax.dynamic_slice_in_dim(q, start, Q_BLOCK, axis=1)
        logits = jnp.einsum('bqhmd,bkhmd->bmhqk', qb, k) * scale
        q_pos = start + jnp.arange(Q_BLOCK)
        causal = k_pos[None, :] <= q_pos[:, None]
        probs = jax.nn.softmax(jnp.where(causal, logits, NEG_INF), axis=-1)
        weights = probs[:, 0] - lam * probs[:, 1]
        return jnp.einsum('bhqk,bkhe->bqhe', weights, vf)

    out = lax.map(one_block, jnp.arange(n_blocks))
    return jnp.moveaxis(out, 0, 1).reshape(B, T, H, vf.shape[-1])


def retention_chunkwise(q, k, v, log_gamma):
    B, T, H, dk = q.shape
    dv = v.shape[-1]
    N = T // RET_CHUNK
    q = q.reshape(B, N, RET_CHUNK, H, dk) * dk ** -0.5
    k = k.reshape(B, N, RET_CHUNK, H, dk)
    v = v.astype(jnp.float32).reshape(B, N, RET_CHUNK, H, dv)
    idx = jnp.arange(RET_CHUNK, dtype=jnp.float32)
    rel = idx[:, None] - idx[None, :]
    intra_decay = jnp.where(rel >= 0, jnp.exp(log_gamma[:, None, None] * jnp.maximum(rel, 0.0)), 0.0)
    scores = jnp.einsum('bnihd,bnjhd->bnhij', q, k) * intra_decay
    intra = jnp.einsum('bnhij,bnjhe->bnihe', scores, v)
    k_decay = jnp.exp(log_gamma[None, :] * (RET_CHUNK - 1 - idx)[:, None])
    chunk_kv = jnp.einsum('bnjhd,jh,bnjhe->nbhde', k, k_decay, v)
    chunk_decay = jnp.exp(log_gamma * RET_CHUNK)[:, None, None]

    def step(state, kv_n):
        return state * chunk_decay + kv_n, state

    _, states = lax.scan(step, jnp.zeros((B, H, dk, dv), jnp.float32), chunk_kv)
    q_decay = jnp.exp(log_gamma[None, :] * (idx + 1.0)[:, None])
    cross = jnp.einsum('bnihd,ih,nbhde->bnihe', q, q_decay, states)
    return (intra + cross).reshape(B, T, H, dv)


def hybrid_mixer(h, layer_idx, w_in, diff_lambda, diff_subln, conv_w, conv_b, conv_ln_g, conv_ln_b,
                 ret_gn_g, w_branch, w_out, diff_cos, diff_sin, ret_cos, ret_sin):
    B, T, D = h.shape
    proj = h @ w_in
    dq, dk, dv, cin, rq, rk, rv, rg, gates = jnp.split(proj, list(np.cumsum(IN_SPLITS)[:-1]), axis=-1)

    q = apply_rope(dq.reshape(B, T, DIFF_HEADS, 2, DIFF_HEAD_DIM), diff_cos, diff_sin)
    k = apply_rope(dk.reshape(B, T, DIFF_HEADS, 2, DIFF_HEAD_DIM), diff_cos, diff_sin)
    lam_f = diff_lambda.astype(jnp.float32)
    lambda_init = 0.8 - 0.6 * math.exp(-0.3 * layer_idx)
    lam = (jnp.exp(jnp.sum(lam_f[0] * lam_f[1])) - jnp.exp(jnp.sum(lam_f[2] * lam_f[3])) + lambda_init)
    ya = diff_attention(q, k, dv.reshape(B, T, DIFF_HEADS, DIFF_V_DIM), lam)
    ya = ya * lax.rsqrt(jnp.mean(ya * ya, axis=-1, keepdims=True) + EPS) * diff_subln.astype(jnp.float32)
    ya = (ya * (1.0 - lambda_init)).reshape(B, T, MIX_W).astype(h.dtype)

    ca, cb = jnp.split(cin, 2, axis=-1)
    u = ca * jax.nn.sigmoid(cb)
    yc = lax.conv_general_dilated(u, conv_w[:, None, :].astype(u.dtype), window_strides=(1,),
                                  padding=[(CONV_WIDTH - 1, 0)],
                                  dimension_numbers=('NWC', 'WIO', 'NWC'),
                                  feature_group_count=CONV_CH)
    yc = layernorm_f32(yc.astype(jnp.float32) + conv_b.astype(jnp.float32), conv_ln_g, conv_ln_b)
    yb = jax.nn.silu(yc).astype(h.dtype)

    log_gamma = jnp.log1p(-jnp.exp2(-5.0 - jnp.arange(RET_HEADS, dtype=jnp.float32)))
    rqr = apply_rope(rq.reshape(B, T, RET_HEADS, RET_HEAD_DIM), ret_cos, ret_sin)
    rkr = apply_rope(rk.reshape(B, T, RET_HEADS, RET_HEAD_DIM), ret_cos, ret_sin)
    yr = retention_chunkwise(rqr, rkr, rv.reshape(B, T, RET_HEADS, RET_HEAD_DIM), log_gamma)
    mu = jnp.mean(yr, axis=-1, keepdims=True)
    var = jnp.mean(jnp.square(yr - mu), axis=-1, keepdims=True)
    yr = ((yr - mu) * lax.rsqrt(var + EPS)).reshape(B, T, MIX_W) * ret_gn_g.astype(jnp.float32)
    yr = (jax.nn.silu(rg.astype(jnp.float32)) * yr).astype(h.dtype)

    g = jax.nn.sigmoid(gates.astype(jnp.float32)).reshape(B, T, N_BRANCH, D)
    merged = (g[:, :, 0] * (ya @ w_branch[0]).astype(jnp.float32)
              + g[:, :, 1] * (yb @ w_branch[1]).astype(jnp.float32)
              + g[:, :, 2] * (yr @ w_branch[2]).astype(jnp.float32))
    return merged.astype(h.dtype) @ w_out


def cross_attention(h, m, wq, wkv, wo):
    B, T, D = h.shape
    M = m.shape[1]
    q = (h @ wq).reshape(B, T, XATTN_HEADS, XATTN_HEAD_DIM).astype(jnp.float32)
    k, v = jnp.split(m @ wkv, 2, axis=-1)
    k = k.reshape(B, M, XATTN_HEADS, XATTN_HEAD_DIM).astype(jnp.float32)
    v = v.reshape(B, M, XATTN_HEADS, XATTN_HEAD_DIM).astype(jnp.float32)
    p = jax.nn.softmax(jnp.einsum('bthd,bmhd->bhtm', q, k) * XATTN_HEAD_DIM ** -0.5, axis=-1)
    o = jnp.einsum('bhtm,bmhd->bthd', p, v).reshape(B, T, D).astype(h.dtype)
    return o @ wo


def swiglu(h, w13, w2):
    a, b = jnp.split(h @ w13, 2, axis=-1)
    return (jax.nn.silu(a.astype(jnp.float32)) * b.astype(jnp.float32)).astype(h.dtype) @ w2


def setup_inputs(seed: int = 0) -> dict:
    key = jax.random.key(seed)
    ks = jax.random.split(key, 24)
    f32 = jnp.float32

    def nrm(k, shape, fan_in):
        return jax.random.normal(k, shape, f32) * fan_in ** -0.5

    def gain(k, shape):
        return 1.0 + 0.02 * jax.random.normal(k, shape, f32)

    return {
        "x": jax.random.normal(ks[0], (BATCH, SEQ, D_MODEL), f32),
        "mem": jax.random.normal(ks[1], (BATCH, MEM_TOKENS, D_MODEL), f32),
        "norm_mix": gain(ks[2], (DEPTH, D_MODEL)),
        "w_in": nrm(ks[3], (DEPTH, D_MODEL, IN_COLS), D_MODEL),
        "diff_lambda": 0.1 * jax.random.normal(ks[4], (DEPTH, 4, DIFF_HEAD_DIM), f32),
        "diff_subln": gain(ks[5], (DEPTH, DIFF_V_DIM)),
        "conv_w": nrm(ks[6], (DEPTH, CONV_WIDTH, CONV_CH), CONV_WIDTH),
        "conv_b": 0.02 * jax.random.normal(ks[7], (DEPTH, CONV_CH), f32),
        "conv_ln_g": gain(ks[8], (DEPTH, CONV_CH)),
        "conv_ln_b": 0.02 * jax.random.normal(ks[9], (DEPTH, CONV_CH), f32),
        "ret_gn_g": gain(ks[10], (DEPTH, MIX_W)),
        "w_branch": nrm(ks[11], (DEPTH, N_BRANCH, MIX_W, D_MODEL), MIX_W),
        "w_out": nrm(ks[12], (DEPTH, D_MODEL, D_MODEL), D_MODEL),
        "norm_xattn": gain(ks[13], (DEPTH, D_MODEL)),
        "norm_mem": gain(ks[14], (DEPTH, D_MODEL)),
        "xattn_wq": nrm(ks[15], (DEPTH, D_MODEL, D_MODEL), D_MODEL),
        "xattn_wkv": nrm(ks[16], (DEPTH, D_MODEL, 2 * D_MODEL), D_MODEL),
        "xattn_wo": nrm(ks[17], (DEPTH, D_MODEL, D_MODEL), D_MODEL),
        "norm_ffn": gain(ks[18], (DEPTH, D_MODEL)),
        "ffn_w13": nrm(ks[19], (DEPTH, D_MODEL, 2 * FFN_HIDDEN), D_MODEL),
        "ffn_w2": nrm(ks[20], (DEPTH, FFN_HIDDEN, D_MODEL), FFN_HIDDEN),
        "norm_final": gain(ks[21], (D_MODEL,)),
    }


def reference(x, mem, norm_mix, w_in, diff_lambda, diff_subln, conv_w, conv_b, conv_ln_g, conv_ln_b,
              ret_gn_g, w_branch, w_out, norm_xattn, norm_mem, xattn_wq, xattn_wkv, xattn_wo,
              norm_ffn, ffn_w13, ffn_w2, norm_final):
    T = x.shape[1]
    diff_inv = ROPE_THETA ** (-jnp.arange(0, DIFF_HEAD_DIM, 2, dtype=jnp.float32) / DIFF_HEAD_DIM)
    diff_cos, diff_sin = rope_tables(T, DIFF_HEAD_DIM, diff_inv)
    ret_inv = 1.0 / (ROPE_THETA ** jnp.linspace(0.0, 1.0, RET_HEAD_DIM // 2, dtype=jnp.float32))
    ret_cos, ret_sin = rope_tables(T, RET_HEAD_DIM, ret_inv)

    for i in range(DEPTH):
        h = rmsnorm(x, norm_mix[i])
        x = x + hybrid_mixer(h, i, w_in[i], diff_lambda[i], diff_subln[i], conv_w[i], conv_b[i],
                             conv_ln_g[i], conv_ln_b[i], ret_gn_g[i], w_branch[i], w_out[i],
                             diff_cos, diff_sin, ret_cos, ret_sin)
        x = x + cross_attention(rmsnorm(x, norm_xattn[i]), rmsnorm(mem, norm_mem[i]),
                                xattn_wq[i], xattn_wkv[i], xattn_wo[i])
        x = x + swiglu(rmsnorm(x, norm_ffn[i]), ffn_w13[i], ffn_w2[i])
    return rmsnorm(x, norm_final)
```

```python
import functools
import math

import jax
import jax.numpy as jnp
from jax import lax
from jax.experimental import pallas as pl
from jax.experimental.pallas import tpu as pltpu

F32 = jnp.float32
BF16 = jnp.bfloat16

DIFF_HEADS = 8
DIFF_HEAD_DIM = 64
RET_HEADS = 8
RET_HEAD_DIM = 128
HEAD_W = 128
CONV_WIDTH = 31
N_BRANCH = 3
XATTN_HEADS = 4
ROPE_THETA = 10000.0
EPS = 1e-6
NEG_INF = -1e30

V7X_VMEM_BYTES = 64 * 1024 * 1024
VMEM_LIMIT_BYTES = V7X_VMEM_BYTES - 12 * 1024 * 1024
LANES = 128


def _params(semantics):
    return pltpu.CompilerParams(dimension_semantics=semantics, vmem_limit_bytes=VMEM_LIMIT_BYTES)


def _rmsnorm_kernel(x_ref, g_ref, o_ref):
    x = x_ref[...]
    y = x * lax.rsqrt(jnp.mean(x * x, axis=-1, keepdims=True) + EPS)
    o_ref[...] = (y * g_ref[...]).astype(o_ref.dtype)


def _rmsnorm(x, g, out_dtype, tm=512):
    m, d = x.shape
    return pl.pallas_call(
        _rmsnorm_kernel,
        out_shape=jax.ShapeDtypeStruct((m, d), out_dtype),
        grid=(m // tm,),
        in_specs=[pl.BlockSpec((tm, d), lambda i: (i, 0)),
                  pl.BlockSpec((1, d), lambda i: (0, 0))],
        out_specs=pl.BlockSpec((tm, d), lambda i: (i, 0)),
        compiler_params=_params(("parallel",)),
        name="rmsnorm",
    )(x, g.reshape(1, d))


def _epi_cast(accs, aux):
    return accs[0]


def _epi_sigmoid(accs, aux):
    return jax.nn.sigmoid(accs[0])


def _epi_silu(accs, aux):
    return jax.nn.silu(accs[0])


def _epi_glu(accs, aux):
    return accs[0] * jax.nn.sigmoid(accs[1])


def _epi_swiglu(accs, aux):
    return jax.nn.silu(accs[0]) * accs[1]


def _epi_residual(accs, aux):
    return aux[0] + accs[0]


def _epi_rope(accs, aux, *, half):
    acc = accs[0]
    cos, sin_signed = aux
    tm, tn = acc.shape
    lane = lax.broadcasted_iota(jnp.int32, (tm, LANES), 1)
    first_half = (lane % (2 * half)) < half
    outs = []
    for c in range(tn // LANES):
        x = acc[:, c * LANES:(c + 1) * LANES]
        if 2 * half == LANES:
            partner = pltpu.roll(x, half, 1)
        else:
            partner = jnp.where(first_half, pltpu.roll(x, LANES - half, 1), pltpu.roll(x, half, 1))
        outs.append(x * cos + partner * sin_signed)
    return jnp.concatenate(outs, axis=1)


def _mm_kernel(*refs, n_w, epilogue):
    a_ref = refs[0]
    w_refs = refs[1:1 + n_w]
    aux_refs = refs[1 + n_w:-1]
    o_ref = refs[-1]
    a = a_ref[...]
    accs = [jnp.dot(a, w[...], preferred_element_type=F32) for w in w_refs]
    o_ref[...] = epilogue(accs, [r[...] for r in aux_refs]).astype(o_ref.dtype)


def _matmul(a, w, col_blocks, n_out, epilogue, out_dtype, aux=(), aux_specs=(), tm=1024, tn=1024):
    m, k = a.shape
    grid = (m // tm, n_out // tn)
    in_specs = [pl.BlockSpec((tm, k), lambda i, j: (i, 0))]
    for cb in col_blocks:
        in_specs.append(pl.BlockSpec((k, tn), lambda i, j, cb=cb: (0, cb(j))))
    in_specs.extend(aux_specs)
    return pl.pallas_call(
        functools.partial(_mm_kernel, n_w=len(col_blocks), epilogue=epilogue),
        out_shape=jax.ShapeDtypeStruct((m, n_out), out_dtype),
        grid=grid,
        in_specs=in_specs,
        out_specs=pl.BlockSpec((tm, tn), lambda i, j: (i, j)),
        compiler_params=_params(("parallel", "arbitrary")),
        name="matmul_" + getattr(epilogue, "__name__", "rope"),
    )(a, *([w] * len(col_blocks)), *aux)


def _matmul_ksplit_kernel(a_ref, w_ref, r_ref, o_ref, acc_ref):
    kk = pl.program_id(2)

    @pl.when(kk == 0)
    def _():
        acc_ref[...] = jnp.zeros_like(acc_ref)

    acc_ref[...] += jnp.dot(a_ref[...], w_ref[...], preferred_element_type=F32)

    @pl.when(kk == pl.num_programs(2) - 1)
    def _():
        o_ref[...] = r_ref[...] + acc_ref[...]


def _matmul_ksplit_residual(a, w, res, tm=1024, tn=1024, tk=1408):
    m, k = a.shape
    n = w.shape[1]
    return pl.pallas_call(
        _matmul_ksplit_kernel,
        out_shape=jax.ShapeDtypeStruct((m, n), F32),
        grid=(m // tm, n // tn, k // tk),
        in_specs=[pl.BlockSpec((tm, tk), lambda i, j, kk: (i, kk)),
                  pl.BlockSpec((tk, tn), lambda i, j, kk: (kk, j)),
                  pl.BlockSpec((tm, tn), lambda i, j, kk: (i, j))],
        out_specs=pl.BlockSpec((tm, tn), lambda i, j, kk: (i, j)),
        scratch_shapes=[pltpu.VMEM((tm, tn), F32)],
        compiler_params=_params(("parallel", "arbitrary", "arbitrary")),
        name="matmul_ksplit_residual",
    )(a, w, res)


def _diff_attn_kernel(lam_ref, sub_ref, q_ref, k_ref, v_ref, o_ref, *, tq, lambda_init):
    qi = pl.program_id(2)
    tk = tq
    scale = DIFF_HEAD_DIM ** -0.5
    q = q_ref[...].astype(F32) * scale
    lane = lax.broadcasted_iota(jnp.int32, (tq, HEAD_W), 1)
    qs = jnp.concatenate([jnp.where(lane < DIFF_HEAD_DIM, q, 0.0),
                          jnp.where(lane >= DIFF_HEAD_DIM, q, 0.0)], axis=0).astype(BF16)

    def block(j, carry, masked):
        m_prev, l_prev, acc_prev = carry
        start = pl.multiple_of(j * tk, tk)
        k = k_ref[pl.ds(start, tk), :]
        v = v_ref[pl.ds(start, tk), :]
        s = lax.dot_general(qs, k, (((1,), (1,)), ((), ())), preferred_element_type=F32)
        if masked:
            row = lax.broadcasted_iota(jnp.int32, (2 * tq, tk), 0)
            row = jnp.where(row >= tq, row - tq, row)
            col = lax.broadcasted_iota(jnp.int32, (2 * tq, tk), 1)
            s = jnp.where(col <= row, s, NEG_INF)
        m_new = jnp.maximum(m_prev, jnp.max(s, axis=-1, keepdims=True))
        alpha = jnp.exp(m_prev - m_new)
        p = jnp.exp(s - m_new)
        l_new = alpha * l_prev + jnp.sum(p, axis=-1, keepdims=True)
        acc_new = alpha * acc_prev + jnp.dot(p.astype(BF16), v, preferred_element_type=F32)
        return m_new, l_new, acc_new

    init = (jnp.full((2 * tq, 1), NEG_INF, F32), jnp.zeros((2 * tq, 1), F32),
            jnp.zeros((2 * tq, HEAD_W), F32))
    carry = lax.fori_loop(0, qi, functools.partial(block, masked=False), init)
    _, l_fin, acc = block(qi, carry, masked=True)

    lam_p = lam_ref[...]
    lam = (jnp.exp(jnp.sum(lam_p[0:1] * lam_p[1:2], axis=-1, keepdims=True))
           - jnp.exp(jnp.sum(lam_p[2:3] * lam_p[3:4], axis=-1, keepdims=True)) + lambda_init)
    o = acc / l_fin
    y = o[:tq] - lam * o[tq:]
    y = y * lax.rsqrt(jnp.mean(y * y, axis=-1, keepdims=True) + EPS) * sub_ref[...]
    o_ref[...] = (y * (1.0 - lambda_init)).astype(o_ref.dtype)


def _diff_attention(qk, v, diff_lambda, diff_subln, batch, seq, lambda_init, tq=256):
    nq = seq // tq
    return pl.pallas_call(
        functools.partial(_diff_attn_kernel, tq=tq, lambda_init=lambda_init),
        out_shape=jax.ShapeDtypeStruct((batch * seq, DIFF_HEADS * HEAD_W), BF16),
        grid=(batch, DIFF_HEADS, nq),
        in_specs=[pl.BlockSpec((4, DIFF_HEAD_DIM), lambda b, h, i: (0, 0)),
                  pl.BlockSpec((1, HEAD_W), lambda b, h, i: (0, 0)),
                  pl.BlockSpec((tq, HEAD_W), lambda b, h, i: (b * nq + i, h)),
                  pl.BlockSpec((seq, HEAD_W), lambda b, h, i: (b, DIFF_HEADS + h)),
                  pl.BlockSpec((seq, HEAD_W), lambda b, h, i: (b, h))],
        out_specs=pl.BlockSpec((tq, HEAD_W), lambda b, h, i: (b * nq + i, h)),
        compiler_params=_params(("parallel", "parallel", "arbitrary")),
        name="diff_attention",
    )(diff_lambda, diff_subln.reshape(1, HEAD_W), qk, qk, v)


CONV_HALO = 32


def _conv_kernel(w_ref, b_ref, g_ref, beta_ref, prev_ref, cur_ref, o_ref, ext_ref, *, tt, rows):
    i = pl.program_id(1)

    @pl.when(i == 0)
    def _():
        ext_ref[0:CONV_HALO, :] = jnp.zeros((CONV_HALO, ext_ref.shape[1]), F32)

    @pl.when(i > 0)
    def _():
        ext_ref[0:CONV_HALO, :] = prev_ref[...]

    ext_ref[CONV_HALO:CONV_HALO + tt, :] = cur_ref[...]
    first = CONV_HALO - (CONV_WIDTH - 1)
    for r0 in range(0, tt, rows):
        acc = jnp.zeros((rows, ext_ref.shape[1]), F32)
        for j in range(CONV_WIDTH):
            acc = acc + w_ref[j:j + 1, :] * ext_ref[r0 + first + j:r0 + first + j + rows, :]
        y = acc + b_ref[...]
        mu = jnp.mean(y, axis=-1, keepdims=True)
        var = jnp.mean(jnp.square(y - mu), axis=-1, keepdims=True)
        y = (y - mu) * lax.rsqrt(var + EPS) * g_ref[...] + beta_ref[...]
        o_ref[r0:r0 + rows, :] = jax.nn.silu(y).astype(o_ref.dtype)


def _conv_module(u, conv_w, conv_b, ln_g, ln_b, batch, seq, tt=128, rows=32):
    ch = u.shape[1]
    nt = seq // tt
    halo_per_tile = tt // CONV_HALO
    vec = lambda a: a.reshape(1, ch)
    row_spec = pl.BlockSpec((1, ch), lambda b, i: (0, 0))
    return pl.pallas_call(
        functools.partial(_conv_kernel, tt=tt, rows=rows),
        out_shape=jax.ShapeDtypeStruct((batch * seq, ch), BF16),
        grid=(batch, nt),
        in_specs=[pl.BlockSpec((CONV_WIDTH, ch), lambda b, i: (0, 0)),
                  row_spec, row_spec, row_spec,
                  pl.BlockSpec((CONV_HALO, ch),
                               lambda b, i: (jnp.maximum((b * nt + i) * halo_per_tile - 1, 0), 0)),
                  pl.BlockSpec((tt, ch), lambda b, i: (b * nt + i, 0))],
        out_specs=pl.BlockSpec((tt, ch), lambda b, i: (b * nt + i, 0)),
        scratch_shapes=[pltpu.VMEM((CONV_HALO + tt, ch), F32)],
        compiler_params=_params(("parallel", "arbitrary")),
        name="conv_module",
    )(conv_w, vec(conv_b), vec(ln_g), vec(ln_b), u, u)


def _retention_kernel(lg_ref, g_ref, q_ref, k_ref, v_ref, gate_ref, o_ref, state_ref, *, chunk, seq):
    lg = lg_ref[0][:, 0:1]
    scale = RET_HEAD_DIM ** -0.5
    row = lax.broadcasted_iota(jnp.int32, (chunk, chunk), 0)
    col = lax.broadcasted_iota(jnp.int32, (chunk, chunk), 1)
    rel = (row - col).astype(F32)
    decay = jnp.where(rel >= 0, jnp.exp(lg * jnp.maximum(rel, 0.0)), 0.0) * scale
    idx = lax.broadcasted_iota(jnp.int32, (chunk, 1), 0).astype(F32)
    q_decay = jnp.exp(lg * (idx + 1.0)) * scale
    k_decay = jnp.exp(lg * (chunk - 1.0 - idx))
    chunk_decay = jnp.exp(lg * chunk)
    state_ref[...] = jnp.zeros_like(state_ref)
    for n in range(seq // chunk):
        sl = slice(n * chunk, (n + 1) * chunk)
        q = q_ref[sl, :]
        k = k_ref[sl, :]
        v = v_ref[sl, :]
        scores = lax.dot_general(q, k, (((1,), (1,)), ((), ())), preferred_element_type=F32) * decay
        y = jnp.dot(scores.astype(BF16), v, preferred_element_type=F32)
        state = state_ref[...]
        y = y + jnp.dot((q.astype(F32) * q_decay).astype(BF16), state.astype(BF16),
                        preferred_element_type=F32)
        kd_t = (k.astype(F32) * k_decay).T.astype(BF16)
        state_ref[...] = state * chunk_decay + jnp.dot(kd_t, v, preferred_element_type=F32)
        mu = jnp.mean(y, axis=-1, keepdims=True)
        var = jnp.mean(jnp.square(y - mu), axis=-1, keepdims=True)
        y = (y - mu) * lax.rsqrt(var + EPS) * g_ref[...]
        o_ref[sl, :] = (gate_ref[sl, :] * y).astype(o_ref.dtype)


def _retention(qk, v, gate, gn_g, batch, seq, chunk=256):
    log_gamma = jnp.log1p(-jnp.exp2(-5.0 - jnp.arange(RET_HEADS, dtype=F32)))
    lg = jnp.broadcast_to(log_gamma[:, None, None], (RET_HEADS, 1, LANES))
    head = lambda b, h: (b, h)
    return pl.pallas_call(
        functools.partial(_retention_kernel, chunk=chunk, seq=seq),
        out_shape=jax.ShapeDtypeStruct((batch * seq, RET_HEADS * HEAD_W), BF16),
        grid=(batch, RET_HEADS),
        in_specs=[pl.BlockSpec((1, 1, LANES), lambda b, h: (h, 0, 0)),
                  pl.BlockSpec((1, HEAD_W), lambda b, h: (0, h)),
                  pl.BlockSpec((seq, HEAD_W), head),
                  pl.BlockSpec((seq, HEAD_W), lambda b, h: (b, RET_HEADS + h)),
                  pl.BlockSpec((seq, HEAD_W), head),
                  pl.BlockSpec((seq, HEAD_W), head)],
        out_specs=pl.BlockSpec((seq, HEAD_W), head),
        scratch_shapes=[pltpu.VMEM((RET_HEAD_DIM, RET_HEAD_DIM), F32)],
        compiler_params=_params(("parallel", "parallel")),
        name="retention",
    )(lg, gn_g.reshape(1, -1), qk, qk, v, gate)


def _merge_kernel(ya_ref, yb_ref, yr_ref, wa_ref, wb_ref, wr_ref, ga_ref, gb_ref, gr_ref, o_ref):
    out = ga_ref[...].astype(F32) * jnp.dot(ya_ref[...], wa_ref[0], preferred_element_type=F32)
    out = out + gb_ref[...].astype(F32) * jnp.dot(yb_ref[...], wb_ref[0], preferred_element_type=F32)
    out = out + gr_ref[...].astype(F32) * jnp.dot(yr_ref[...], wr_ref[0], preferred_element_type=F32)
    o_ref[...] = out.astype(o_ref.dtype)


def _merge(ya, yb, yr, w_branch, gates, tm=1024, tn=1024):
    m, kdim = ya.shape
    d = w_branch.shape[2]
    nb = d // tn
    y_spec = pl.BlockSpec((tm, kdim), lambda i, j: (i, 0))
    w_spec = lambda b: pl.BlockSpec((1, kdim, tn), lambda i, j: (b, 0, j))
    g_spec = lambda b: pl.BlockSpec((tm, tn), lambda i, j: (i, b * nb + j))
    return pl.pallas_call(
        _merge_kernel,
        out_shape=jax.ShapeDtypeStruct((m, d), BF16),
        grid=(m // tm, nb),
        in_specs=[y_spec, y_spec, y_spec, w_spec(0), w_spec(1), w_spec(2),
                  g_spec(0), g_spec(1), g_spec(2)],
        out_specs=pl.BlockSpec((tm, tn), lambda i, j: (i, j)),
        compiler_params=_params(("parallel", "arbitrary")),
        name="branch_merge",
    )(ya, yb, yr, w_branch, w_branch, w_branch, gates, gates, gates)


def _xattn_kernel(q_ref, k_ref, v_ref, o_ref, *, scale):
    s = lax.dot_general(q_ref[...], k_ref[...], (((1,), (1,)), ((), ())),
                        preferred_element_type=F32) * scale
    p = jnp.exp(s - jnp.max(s, axis=-1, keepdims=True))
    l = jnp.sum(p, axis=-1, keepdims=True)
    o = jnp.dot(p.astype(BF16), v_ref[...], preferred_element_type=F32) / l
    o_ref[...] = o.astype(o_ref.dtype)


def _cross_attention(q, kv, batch, seq, mem_tokens, tq=512):
    d = q.shape[1]
    hd = d // XATTN_HEADS
    nq = seq // tq
    return pl.pallas_call(
        functools.partial(_xattn_kernel, scale=hd ** -0.5),
        out_shape=jax.ShapeDtypeStruct((batch * seq, d), BF16),
        grid=(batch, XATTN_HEADS, nq),
        in_specs=[pl.BlockSpec((tq, hd), lambda b, h, i: (b * nq + i, h)),
                  pl.BlockSpec((mem_tokens, hd), lambda b, h, i: (b, h)),
                  pl.BlockSpec((mem_tokens, hd), lambda b, h, i: (b, XATTN_HEADS + h))],
        out_specs=pl.BlockSpec((tq, hd), lambda b, h, i: (b * nq + i, h)),
        compiler_params=_params(("parallel", "parallel", "arbitrary")),
        name="cross_attention",
    )(q, kv, kv)


def _rope_tables(seq, half, inv_freq):
    ang = jnp.arange(seq, dtype=F32)[:, None] * inv_freq[None, :]
    cos, sin = jnp.cos(ang), jnp.sin(ang)
    reps = LANES // (2 * half)
    cos_l = jnp.tile(jnp.concatenate([cos, cos], axis=1), (1, reps))
    sin_l = jnp.tile(jnp.concatenate([-sin, sin], axis=1), (1, reps))
    return cos_l, sin_l


def kernel(x, mem, norm_mix, w_in, diff_lambda, diff_subln, conv_w, conv_b, conv_ln_g, conv_ln_b,
           ret_gn_g, w_branch, w_out, norm_xattn, norm_mem, xattn_wq, xattn_wkv, xattn_wo,
           norm_ffn, ffn_w13, ffn_w2, norm_final):
    batch, seq, d = x.shape
    mem_tokens = mem.shape[1]
    depth = w_in.shape[0]
    mix_w = d // 2
    ffn_hidden = ffn_w2.shape[1]
    tm = 1024
    tn = 1024
    rows_per_seq = seq // tm

    diff_inv = ROPE_THETA ** (-jnp.arange(0, DIFF_HEAD_DIM, 2, dtype=F32) / DIFF_HEAD_DIM)
    diff_tabs = _rope_tables(seq, DIFF_HEAD_DIM // 2, diff_inv)
    ret_inv = 1.0 / (ROPE_THETA ** jnp.linspace(0.0, 1.0, RET_HEAD_DIM // 2, dtype=F32))
    ret_tabs = _rope_tables(seq, RET_HEAD_DIM // 2, ret_inv)
    tab_spec = pl.BlockSpec((tm, LANES), lambda i, j: (i % rows_per_seq, 0))

    xf = x.reshape(batch * seq, d)
    memf = mem.reshape(batch * mem_tokens, d)
    blk = mix_w // tn
    col = {"dq": 0, "dv": 2 * blk, "ca": 3 * blk, "cb": 4 * blk, "rq": 5 * blk, "rv": 7 * blk,
           "rg": 8 * blk, "gates": 9 * blk}

    for i in range(depth):
        lambda_init = 0.8 - 0.6 * math.exp(-0.3 * i)
        w_in_b = w_in[i].astype(BF16)
        h = _rmsnorm(xf, norm_mix[i], BF16)

        off = lambda name: (lambda j, o=col[name]: o + j)
        dqk = _matmul(h, w_in_b, [off("dq")], 2 * mix_w,
                      functools.partial(_epi_rope, half=DIFF_HEAD_DIM // 2), BF16,
                      aux=diff_tabs, aux_specs=(tab_spec, tab_spec), tm=tm, tn=tn)
        rqk = _matmul(h, w_in_b, [off("rq")], 2 * mix_w,
                      functools.partial(_epi_rope, half=RET_HEAD_DIM // 2), BF16,
                      aux=ret_tabs, aux_specs=(tab_spec, tab_spec), tm=tm, tn=tn)
        dv = _matmul(h, w_in_b, [off("dv")], mix_w, _epi_cast, BF16, tm=tm, tn=tn)
        rv = _matmul(h, w_in_b, [off("rv")], mix_w, _epi_cast, BF16, tm=tm, tn=tn)
        u = _matmul(h, w_in_b, [off("ca"), off("cb")], mix_w, _epi_glu, F32, tm=tm, tn=tn)
        rgate = _matmul(h, w_in_b, [off("rg")], mix_w, _epi_silu, F32, tm=tm, tn=tn)
        gates = _matmul(h, w_in_b, [off("gates")], N_BRANCH * d, _epi_sigmoid, BF16, tm=tm, tn=tn)

        ya = _diff_attention(dqk, dv, diff_lambda[i], diff_subln[i], batch, seq, lambda_init)
        yb = _conv_module(u, conv_w[i], conv_b[i], conv_ln_g[i], conv_ln_b[i], batch, seq)
        yr = _retention(rqk, rv, rgate, ret_gn_g[i], batch, seq)

        merged = _merge(ya, yb, yr, w_branch[i].astype(BF16), gates, tm=tm, tn=tn)
        res_spec = pl.BlockSpec((tm, tn), lambda i_, j: (i_, j))
        xf = _matmul(merged, w_out[i].astype(BF16), [lambda j: j], d, _epi_residual, F32,
                     aux=(xf,), aux_specs=(res_spec,), tm=tm, tn=tn)

        hq = _rmsnorm(xf, norm_xattn[i], BF16)
        hm = _rmsnorm(memf, norm_mem[i], BF16, tm=256)
        q = _matmul(hq, xattn_wq[i].astype(BF16), [lambda j: j], d, _epi_cast, BF16, tm=tm, tn=tn)
        kv = _matmul(hm, xattn_wkv[i].astype(BF16), [lambda j: j], 2 * d, _epi_cast, BF16,
                     tm=min(tm, batch * mem_tokens), tn=tn)
        o = _cross_attention(q, kv, batch, seq, mem_tokens)
        xf = _matmul(o, xattn_wo[i].astype(BF16), [lambda j: j], d, _epi_residual, F32,
                     aux=(xf,), aux_specs=(res_spec,), tm=tm, tn=tn)

        hf = _rmsnorm(xf, norm_ffn[i], BF16)
        tn_f = 512
        nbf = ffn_hidden // tn_f
        act = _matmul(hf, ffn_w13[i].astype(BF16), [lambda j: j, lambda j, o=nbf: o + j], ffn_hidden,
                      _epi_swiglu, BF16, tm=tm, tn=tn_f)
        xf = _matmul_ksplit_residual(act, ffn_w2[i].astype(BF16), xf, tm=tm, tn=tn,
                                     tk=ffn_hidden // 4)

    out = _rmsnorm(xf, norm_final, F32)
    return out.reshape(batch, seq, d)
```

```python
import functools
import math

import jax
import jax.numpy as jnp
from jax import lax
from jax.experimental import pallas as pl
from jax.experimental.pallas import tpu as pltpu

F32 = jnp.float32
BF16 = jnp.bfloat16

DIFF_HEADS = 8
DIFF_HEAD_DIM = 64
RET_HEADS = 8
RET_HEAD_DIM = 128
HEAD_W = 128
CONV_WIDTH = 31
N_BRANCH = 3
XATTN_HEADS = 4
ROPE_THETA = 10000.0
EPS = 1e-6
NEG_INF = -1e30

V7X_VMEM_BYTES = 64 * 1024 * 1024
VMEM_LIMIT_BYTES = V7X_VMEM_BYTES - 12 * 1024 * 1024
LANES = 128
SUBLANES = 8


def _params(semantics):
    return pltpu.CompilerParams(dimension_semantics=semantics, vmem_limit_bytes=VMEM_LIMIT_BYTES)


def _rmsnorm_kernel(x_ref, g_ref, o_ref):
    x = x_ref[...]
    y = x * lax.rsqrt(jnp.mean(x * x, axis=-1, keepdims=True) + EPS)
    o_ref[...] = (y * g_ref[...]).astype(o_ref.dtype)


def _rmsnorm(x, g, out_dtype, tm=512):
    m, d = x.shape
    return pl.pallas_call(
        _rmsnorm_kernel,
        out_shape=jax.ShapeDtypeStruct((m, d), out_dtype),
        grid=(m // tm,),
        in_specs=[pl.BlockSpec((tm, d), lambda i: (i, 0)),
                  pl.BlockSpec((1, d), lambda i: (0, 0))],
        out_specs=pl.BlockSpec((tm, d), lambda i: (i, 0)),
        compiler_params=_params(("parallel",)),
        name="rmsnorm",
    )(x, g.reshape(1, d))


def _epi_cast(accs, aux):
    return accs[0]


def _epi_sigmoid(accs, aux):
    return jax.nn.sigmoid(accs[0])


def _epi_silu(accs, aux):
    return jax.nn.silu(accs[0])


def _epi_glu(accs, aux):
    return accs[0] * jax.nn.sigmoid(accs[1])


def _epi_swiglu(accs, aux):
    return jax.nn.silu(accs[0]) * accs[1]


def _epi_residual(accs, aux):
    return aux[0] + accs[0]


def _epi_rope(accs, aux, *, half):
    acc = accs[0]
    cos, sin_signed = aux
    tm, tn = acc.shape
    lane = lax.broadcasted_iota(jnp.int32, (tm, LANES), 1)
    first_half = (lane % (2 * half)) < half
    outs = []
    for c in range(tn // LANES):
        x = acc[:, c * LANES:(c + 1) * LANES]
        if 2 * half == LANES:
            partner = pltpu.roll(x, half, 1)
        else:
            partner = jnp.where(first_half, pltpu.roll(x, LANES - half, 1), pltpu.roll(x, half, 1))
        outs.append(x * cos + partner * sin_signed)
    return jnp.concatenate(outs, axis=1)


def _mm_kernel(*refs, n_w, epilogue):
    a_ref = refs[0]
    w_refs = refs[1:1 + n_w]
    aux_refs = refs[1 + n_w:-1 - n_w]
    o_ref = refs[-1 - n_w]
    wb_refs = refs[-n_w:]

    @pl.when(pl.program_id(1) == 0)
    def _():
        for w, wb in zip(w_refs, wb_refs):
            wb[...] = w[0].astype(BF16)

    a = a_ref[...]
    accs = [jnp.dot(a, wb[...], preferred_element_type=F32) for wb in wb_refs]
    o_ref[...] = epilogue(accs, [r[...] for r in aux_refs]).astype(o_ref.dtype)


def _matmul(a, w, layer, col_blocks, n_out, epilogue, out_dtype, aux=(), aux_specs=(), tm=1024, tn=1024):
    m, k = a.shape
    n_w = len(col_blocks)
    in_specs = [pl.BlockSpec((tm, k), lambda j, i: (i, 0))]
    for cb in col_blocks:
        in_specs.append(pl.BlockSpec((1, k, tn), lambda j, i, cb=cb: (layer, 0, cb(j))))
    in_specs.extend(aux_specs)
    return pl.pallas_call(
        functools.partial(_mm_kernel, n_w=n_w, epilogue=epilogue),
        out_shape=jax.ShapeDtypeStruct((m, n_out), out_dtype),
        grid=(n_out // tn, m // tm),
        in_specs=in_specs,
        out_specs=pl.BlockSpec((tm, tn), lambda j, i: (i, j)),
        scratch_shapes=[pltpu.VMEM((k, tn), BF16) for _ in range(n_w)],
        compiler_params=_params(("arbitrary", "arbitrary")),
        name="matmul_" + getattr(epilogue, "__name__", "rope"),
    )(a, *([w] * n_w), *aux)


def _matmul_ksplit_kernel(a_ref, w_ref, r_ref, o_ref, acc_ref):
    kk = pl.program_id(2)

    @pl.when(kk == 0)
    def _():
        acc_ref[...] = jnp.zeros_like(acc_ref)

    acc_ref[...] += jnp.dot(a_ref[...], w_ref[...], preferred_element_type=F32)

    @pl.when(kk == pl.num_programs(2) - 1)
    def _():
        o_ref[...] = r_ref[...] + acc_ref[...]


def _matmul_ksplit_residual(a, w, res, tm=1024, tn=1024, tk=1408):
    m, k = a.shape
    n = w.shape[1]
    return pl.pallas_call(
        _matmul_ksplit_kernel,
        out_shape=jax.ShapeDtypeStruct((m, n), F32),
        grid=(m // tm, n // tn, k // tk),
        in_specs=[pl.BlockSpec((tm, tk), lambda i, j, kk: (i, kk)),
                  pl.BlockSpec((tk, tn), lambda i, j, kk: (kk, j)),
                  pl.BlockSpec((tm, tn), lambda i, j, kk: (i, j))],
        out_specs=pl.BlockSpec((tm, tn), lambda i, j, kk: (i, j)),
        scratch_shapes=[pltpu.VMEM((tm, tn), F32)],
        compiler_params=_params(("parallel", "arbitrary", "arbitrary")),
        name="matmul_ksplit_residual",
    )(a, w, res)


def _diff_attn_kernel(lam_ref, sub_ref, q_ref, k_ref, v_ref, o_ref, vt_ref, acc_ref, *, tq, group,
                      lambda_init):
    qi = pl.program_id(2)
    tk = tq
    nkv = v_ref.shape[0] // tk
    heads = [slice(g * HEAD_W, (g + 1) * HEAD_W) for g in range(group)]

    @pl.when(qi == 0)
    def _():
        for g in range(group):
            for jb in range(nkv):
                vt_ref[g, jb] = v_ref[jb * tk:(jb + 1) * tk, heads[g]].astype(F32).T.astype(BF16)

    scale = DIFF_HEAD_DIM ** -0.5 * math.log2(math.e)
    dim = lax.broadcasted_iota(jnp.int32, (HEAD_W, tq), 0)
    qsts = []
    for g in range(group):
        qt = (q_ref[:, heads[g]].astype(F32) * scale).T
        qsts.append(jnp.concatenate([jnp.where(dim < DIFF_HEAD_DIM, qt, 0.0),
                                     jnp.where(dim >= DIFF_HEAD_DIM, qt, 0.0)], axis=1).astype(BF16))

    def block(j, carry, masked):
        new = []
        logits = [jnp.dot(k_ref[pl.ds(pl.multiple_of(j * tk, tk), tk), heads[g]], qsts[g],
                          preferred_element_type=F32) for g in range(group)]
        for g in range(group):
            m_prev, l_prev = carry[g]
            s = logits[g]
            if masked:
                key = lax.broadcasted_iota(jnp.int32, (tk, 2 * tq), 0)
                qry = lax.broadcasted_iota(jnp.int32, (tk, 2 * tq), 1)
                qry = jnp.where(qry >= tq, qry - tq, qry)
                s = jnp.where(key <= qry, s, NEG_INF)
            m_new = jnp.maximum(m_prev, jnp.max(s, axis=0, keepdims=True))
            alpha = jnp.exp2(m_prev - m_new)
            p = jnp.exp2(s - m_new)
            l_new = alpha * l_prev + jnp.sum(p, axis=0, keepdims=True)
            acc_ref[g] = alpha * acc_ref[g] + jnp.dot(vt_ref[g, j], p.astype(BF16),
                                                      preferred_element_type=F32)
            new.append((m_new, l_new))
        return tuple(new)

    acc_ref[...] = jnp.zeros_like(acc_ref)
    init = tuple((jnp.full((1, 2 * tq), NEG_INF, F32), jnp.zeros((1, 2 * tq), F32)) for _ in range(group))
    carry = lax.fori_loop(0, qi, functools.partial(block, masked=False), init)
    carry = block(qi, carry, masked=True)

    lam_p = lam_ref[...]
    lam = (jnp.exp(jnp.sum(lam_p[0:1] * lam_p[1:2], axis=-1, keepdims=True))
           - jnp.exp(jnp.sum(lam_p[2:3] * lam_p[3:4], axis=-1, keepdims=True)) + lambda_init)
    for g in range(group):
        o = acc_ref[g] * (1.0 / carry[g][1])
        y = o[:, :tq] - lam * o[:, tq:]
        y = y * lax.rsqrt(jnp.mean(y * y, axis=0, keepdims=True) + EPS) * sub_ref[...]
        o_ref[:, heads[g]] = (y * (1.0 - lambda_init)).T.astype(o_ref.dtype)


def _diff_attention(qk, v, diff_lambda, diff_subln, batch, seq, lambda_init, tq=256, group=4):
    nq = seq // tq
    n_groups = DIFF_HEADS // group
    gw = group * HEAD_W
    return pl.pallas_call(
        functools.partial(_diff_attn_kernel, tq=tq, group=group, lambda_init=lambda_init),
        out_shape=jax.ShapeDtypeStruct((batch * seq, DIFF_HEADS * HEAD_W), BF16),
        grid=(batch, n_groups, nq),
        in_specs=[pl.BlockSpec((4, DIFF_HEAD_DIM), lambda b, h, i: (0, 0)),
                  pl.BlockSpec((HEAD_W, 1), lambda b, h, i: (0, 0)),
                  pl.BlockSpec((tq, gw), lambda b, h, i: (b * nq + i, h)),
                  pl.BlockSpec((seq, gw), lambda b, h, i: (b, n_groups + h)),
                  pl.BlockSpec((seq, gw), lambda b, h, i: (b, h))],
        out_specs=pl.BlockSpec((tq, gw), lambda b, h, i: (b * nq + i, h)),
        scratch_shapes=[pltpu.VMEM((group, nq, HEAD_W, tq), BF16),
                        pltpu.VMEM((group, HEAD_W, 2 * tq), F32)],
        compiler_params=_params(("parallel", "parallel", "arbitrary")),
        name="diff_attention",
    )(diff_lambda, diff_subln.reshape(HEAD_W, 1), qk, qk, v)


CONV_HALO = 32
LN_ROWS = 64


def _conv_kernel(w_ref, b_ref, g_ref, beta_ref, prev_ref, cur_ref, o_ref, sh_ref, y_ref, *, tt, rows,
                 lanes):
    i = pl.program_id(1)
    ch = sh_ref.shape[2]
    n_ext = CONV_HALO + tt

    @pl.when(i == 0)
    def _():
        sh_ref[0, 0:CONV_HALO, :] = jnp.zeros((CONV_HALO, ch), F32)

    @pl.when(i > 0)
    def _():
        sh_ref[0, 0:CONV_HALO, :] = prev_ref[...]

    sh_ref[0, CONV_HALO:n_ext, :] = cur_ref[...]
    for c0 in range(0, ch, LANES):
        ext = sh_ref[0, :, c0:c0 + LANES]
        for s in range(1, SUBLANES):
            sh_ref[s, 0:n_ext - SUBLANES, c0:c0 + LANES] = pltpu.roll(ext, n_ext - s, 0)[0:n_ext - SUBLANES]

    first = CONV_HALO - (CONV_WIDTH - 1)
    groups = rows // SUBLANES

    def row_chunk(c, carry):
        r0 = pl.multiple_of(c * rows, rows)
        for c0 in range(0, ch, lanes):
            acc = jnp.zeros((groups, SUBLANES, lanes), F32)
            for s in range(SUBLANES):
                offs = [o for o in range(first, first + CONV_WIDTH) if o % SUBLANES == s]
                a_lo, a_hi = offs[0] // SUBLANES, offs[-1] // SUBLANES
                span = sh_ref[s, pl.ds(r0 + SUBLANES * a_lo, rows + SUBLANES * (a_hi - a_lo)), c0:c0 + lanes]
                span = span.reshape(groups + a_hi - a_lo, SUBLANES, lanes)
                for o in offs:
                    a = o // SUBLANES - a_lo
                    acc = acc + w_ref[o - first, :, c0:c0 + lanes] * span[a:a + groups]
            y_ref[pl.ds(r0, rows), c0:c0 + lanes] = acc.reshape(rows, lanes)
        return carry

    lax.fori_loop(0, tt // rows, row_chunk, 0)

    for r0 in range(0, tt, LN_ROWS):
        y = y_ref[r0:r0 + LN_ROWS, :] + b_ref[...]
        mu = jnp.mean(y, axis=-1, keepdims=True)
        var = jnp.mean(jnp.square(y - mu), axis=-1, keepdims=True)
        y = (y - mu) * lax.rsqrt(var + EPS) * g_ref[...] + beta_ref[...]
        o_ref[r0:r0 + LN_ROWS, :] = jax.nn.silu(y).astype(o_ref.dtype)


def _conv_module(u, conv_w, conv_b, ln_g, ln_b, batch, seq, tt=256, rows=32, lanes=256):
    ch = u.shape[1]
    conv_w = jnp.broadcast_to(conv_w[:, None, :], (CONV_WIDTH, SUBLANES, ch))
    nt = seq // tt
    halo_per_tile = tt // CONV_HALO
    vec = lambda a: a.reshape(1, ch)
    row_spec = pl.BlockSpec((1, ch), lambda b, i: (0, 0))
    return pl.pallas_call(
        functools.partial(_conv_kernel, tt=tt, rows=rows, lanes=lanes),
        out_shape=jax.ShapeDtypeStruct((batch * seq, ch), BF16),
        grid=(batch, nt),
        in_specs=[pl.BlockSpec((CONV_WIDTH, SUBLANES, ch), lambda b, i: (0, 0, 0)),
                  row_spec, row_spec, row_spec,
                  pl.BlockSpec((CONV_HALO, ch),
                               lambda b, i: (jnp.maximum((b * nt + i) * halo_per_tile - 1, 0), 0)),
                  pl.BlockSpec((tt, ch), lambda b, i: (b * nt + i, 0))],
        out_specs=pl.BlockSpec((tt, ch), lambda b, i: (b * nt + i, 0)),
        scratch_shapes=[pltpu.VMEM((SUBLANES, CONV_HALO + tt, ch), F32), pltpu.VMEM((tt, ch), F32)],
        compiler_params=_params(("parallel", "arbitrary")),
        name="conv_module",
    )(conv_w, vec(conv_b), vec(ln_g), vec(ln_b), u, u)


def _retention_kernel(lg_ref, g_ref, q_ref, k_ref, v_ref, gate_ref, o_ref, state_ref, *, chunk, seq):
    lg = lg_ref[0][:, 0:1]
    scale = RET_HEAD_DIM ** -0.5
    row = lax.broadcasted_iota(jnp.int32, (chunk, chunk), 0)
    col = lax.broadcasted_iota(jnp.int32, (chunk, chunk), 1)
    rel = (row - col).astype(F32)
    decay = jnp.where(rel >= 0, jnp.exp(lg * jnp.maximum(rel, 0.0)), 0.0) * scale
    idx = lax.broadcasted_iota(jnp.int32, (chunk, 1), 0).astype(F32)
    q_decay = jnp.exp(lg * (idx + 1.0)) * scale
    k_decay = jnp.exp(lg * (chunk - 1.0 - idx))
    chunk_decay = jnp.exp(lg * chunk)
    state_ref[...] = jnp.zeros_like(state_ref)
    for n in range(seq // chunk):
        sl = slice(n * chunk, (n + 1) * chunk)
        q = q_ref[sl, :]
        k = k_ref[sl, :]
        v = v_ref[sl, :]
        scores = lax.dot_general(q, k, (((1,), (1,)), ((), ())), preferred_element_type=F32) * decay
        y = jnp.dot(scores.astype(BF16), v, preferred_element_type=F32)
        state = state_ref[...]
        y = y + jnp.dot((q.astype(F32) * q_decay).astype(BF16), state.astype(BF16),
                        preferred_element_type=F32)
        kd_t = (k.astype(F32) * k_decay).T.astype(BF16)
        state_ref[...] = state * chunk_decay + jnp.dot(kd_t, v, preferred_element_type=F32)
        mu = jnp.mean(y, axis=-1, keepdims=True)
        var = jnp.mean(jnp.square(y - mu), axis=-1, keepdims=True)
        y = (y - mu) * lax.rsqrt(var + EPS) * g_ref[...]
        o_ref[sl, :] = (gate_ref[sl, :] * y).astype(o_ref.dtype)


def _retention(qk, v, gate, gn_g, batch, seq, chunk=256):
    log_gamma = jnp.log1p(-jnp.exp2(-5.0 - jnp.arange(RET_HEADS, dtype=F32)))
    lg = jnp.broadcast_to(log_gamma[:, None, None], (RET_HEADS, 1, LANES))
    head = lambda b, h: (b, h)
    return pl.pallas_call(
        functools.partial(_retention_kernel, chunk=chunk, seq=seq),
        out_shape=jax.ShapeDtypeStruct((batch * seq, RET_HEADS * HEAD_W), BF16),
        grid=(batch, RET_HEADS),
        in_specs=[pl.BlockSpec((1, 1, LANES), lambda b, h: (h, 0, 0)),
                  pl.BlockSpec((1, HEAD_W), lambda b, h: (0, h)),
                  pl.BlockSpec((seq, HEAD_W), head),
                  pl.BlockSpec((seq, HEAD_W), lambda b, h: (b, RET_HEADS + h)),
                  pl.BlockSpec((seq, HEAD_W), head),
                  pl.BlockSpec((seq, HEAD_W), head)],
        out_specs=pl.BlockSpec((seq, HEAD_W), head),
        scratch_shapes=[pltpu.VMEM((RET_HEAD_DIM, RET_HEAD_DIM), F32)],
        compiler_params=_params(("parallel", "parallel")),
        name="retention",
    )(lg, gn_g.reshape(1, -1), qk, qk, v, gate)


def _merge_kernel(ya_ref, yb_ref, yr_ref, wa_ref, wb_ref, wr_ref, ga_ref, gb_ref, gr_ref, o_ref):
    out = ga_ref[...].astype(F32) * jnp.dot(ya_ref[...], wa_ref[0], preferred_element_type=F32)
    out = out + gb_ref[...].astype(F32) * jnp.dot(yb_ref[...], wb_ref[0], preferred_element_type=F32)
    out = out + gr_ref[...].astype(F32) * jnp.dot(yr_ref[...], wr_ref[0], preferred_element_type=F32)
    o_ref[...] = out.astype(o_ref.dtype)


def _merge(ya, yb, yr, w_branch, gates, tm=1024, tn=1024):
    m, kdim = ya.shape
    d = w_branch.shape[2]
    nb = d // tn
    y_spec = pl.BlockSpec((tm, kdim), lambda i, j: (i, 0))
    w_spec = lambda b: pl.BlockSpec((1, kdim, tn), lambda i, j: (b, 0, j))
    g_spec = lambda b: pl.BlockSpec((tm, tn), lambda i, j: (i, b * nb + j))
    return pl.pallas_call(
        _merge_kernel,
        out_shape=jax.ShapeDtypeStruct((m, d), BF16),
        grid=(m // tm, nb),
        in_specs=[y_spec, y_spec, y_spec, w_spec(0), w_spec(1), w_spec(2),
                  g_spec(0), g_spec(1), g_spec(2)],
        out_specs=pl.BlockSpec((tm, tn), lambda i, j: (i, j)),
        compiler_params=_params(("parallel", "arbitrary")),
        name="branch_merge",
    )(ya, yb, yr, w_branch, w_branch, w_branch, gates, gates, gates)


def _xattn_kernel(q_ref, k_ref, v_ref, o_ref, *, scale):
    s = lax.dot_general(q_ref[...], k_ref[...], (((1,), (1,)), ((), ())),
                        preferred_element_type=F32) * scale
    p = jnp.exp(s - jnp.max(s, axis=-1, keepdims=True))
    l = jnp.sum(p, axis=-1, keepdims=True)
    o = jnp.dot(p.astype(BF16), v_ref[...], preferred_element_type=F32) / l
    o_ref[...] = o.astype(o_ref.dtype)


def _cross_attention(q, kv, batch, seq, mem_tokens, tq=512):
    d = q.shape[1]
    hd = d // XATTN_HEADS
    nq = seq // tq
    return pl.pallas_call(
        functools.partial(_xattn_kernel, scale=hd ** -0.5),
        out_shape=jax.ShapeDtypeStruct((batch * seq, d), BF16),
        grid=(batch, XATTN_HEADS, nq),
        in_specs=[pl.BlockSpec((tq, hd), lambda b, h, i: (b * nq + i, h)),
                  pl.BlockSpec((mem_tokens, hd), lambda b, h, i: (b, h)),
                  pl.BlockSpec((mem_tokens, hd), lambda b, h, i: (b, XATTN_HEADS + h))],
        out_specs=pl.BlockSpec((tq, hd), lambda b, h, i: (b * nq + i, h)),
        compiler_params=_params(("parallel", "parallel", "arbitrary")),
        name="cross_attention",
    )(q, kv, kv)


def _rope_tables(seq, half, inv_freq):
    ang = jnp.arange(seq, dtype=F32)[:, None] * inv_freq[None, :]
    cos, sin = jnp.cos(ang), jnp.sin(ang)
    reps = LANES // (2 * half)
    cos_l = jnp.tile(jnp.concatenate([cos, cos], axis=1), (1, reps))
    sin_l = jnp.tile(jnp.concatenate([-sin, sin], axis=1), (1, reps))
    return cos_l, sin_l


def kernel(x, mem, norm_mix, w_in, diff_lambda, diff_subln, conv_w, conv_b, conv_ln_g, conv_ln_b,
           ret_gn_g, w_branch, w_out, norm_xattn, norm_mem, xattn_wq, xattn_wkv, xattn_wo,
           norm_ffn, ffn_w13, ffn_w2, norm_final):
    batch, seq, d = x.shape
    mem_tokens = mem.shape[1]
    depth = w_in.shape[0]
    mix_w = d // 2
    ffn_hidden = ffn_w2.shape[1]
    tm = 1024
    tn = 1024
    rows_per_seq = seq // tm

    diff_inv = ROPE_THETA ** (-jnp.arange(0, DIFF_HEAD_DIM, 2, dtype=F32) / DIFF_HEAD_DIM)
    diff_tabs = _rope_tables(seq, DIFF_HEAD_DIM // 2, diff_inv)
    ret_inv = 1.0 / (ROPE_THETA ** jnp.linspace(0.0, 1.0, RET_HEAD_DIM // 2, dtype=F32))
    ret_tabs = _rope_tables(seq, RET_HEAD_DIM // 2, ret_inv)
    tab_spec = pl.BlockSpec((tm, LANES), lambda j, i: (i % rows_per_seq, 0))
    res_spec = pl.BlockSpec((tm, tn), lambda j, i: (i, j))
    tn_glu = 512

    xf = x.reshape(batch * seq, d)
    memf = mem.reshape(batch * mem_tokens, d)
    seg = {"dq": 0, "dv": 2 * mix_w, "ca": 3 * mix_w, "cb": 4 * mix_w, "rq": 5 * mix_w, "rv": 7 * mix_w,
           "rg": 8 * mix_w, "gates": 9 * mix_w}

    def off(name, width=tn):
        return lambda j, o=seg[name] // width: o + j

    ident = lambda j: j

    for i in range(depth):
        lambda_init = 0.8 - 0.6 * math.exp(-0.3 * i)
        h = _rmsnorm(xf, norm_mix[i], BF16)

        dqk = _matmul(h, w_in, i, [off("dq")], 2 * mix_w,
                      functools.partial(_epi_rope, half=DIFF_HEAD_DIM // 2), BF16,
                      aux=diff_tabs, aux_specs=(tab_spec, tab_spec), tm=tm, tn=tn)
        rqk = _matmul(h, w_in, i, [off("rq")], 2 * mix_w,
                      functools.partial(_epi_rope, half=RET_HEAD_DIM // 2), BF16,
                      aux=ret_tabs, aux_specs=(tab_spec, tab_spec), tm=tm, tn=tn)
        dv = _matmul(h, w_in, i, [off("dv")], mix_w, _epi_cast, BF16, tm=tm, tn=tn)
        rv = _matmul(h, w_in, i, [off("rv")], mix_w, _epi_cast, BF16, tm=tm, tn=tn)
        u = _matmul(h, w_in, i, [off("ca", tn_glu), off("cb", tn_glu)], mix_w, _epi_glu, F32,
                    tm=tm, tn=tn_glu)
        rgate = _matmul(h, w_in, i, [off("rg")], mix_w, _epi_silu, F32, tm=tm, tn=tn)
        gates = _matmul(h, w_in, i, [off("gates")], N_BRANCH * d, _epi_sigmoid, BF16, tm=tm, tn=tn)

        ya = _diff_attention(dqk, dv, diff_lambda[i], diff_subln[i], batch, seq, lambda_init)
        yb = _conv_module(u, conv_w[i], conv_b[i], conv_ln_g[i], conv_ln_b[i], batch, seq)
        yr = _retention(rqk, rv, rgate, ret_gn_g[i], batch, seq)

        merged = _merge(ya, yb, yr, w_branch[i].astype(BF16), gates, tm=tm, tn=tn)
        xf = _matmul(merged, w_out, i, [ident], d, _epi_residual, F32,
                     aux=(xf,), aux_specs=(res_spec,), tm=tm, tn=tn)

        hq = _rmsnorm(xf, norm_xattn[i], BF16)
        hm = _rmsnorm(memf, norm_mem[i], BF16, tm=256)
        q = _matmul(hq, xattn_wq, i, [ident], d, _epi_cast, BF16, tm=tm, tn=tn)
        kv = _matmul(hm, xattn_wkv, i, [ident], 2 * d, _epi_cast, BF16,
                     tm=min(tm, batch * mem_tokens), tn=tn)
        o = _cross_attention(q, kv, batch, seq, mem_tokens)
        xf = _matmul(o, xattn_wo, i, [ident], d, _epi_residual, F32,
                     aux=(xf,), aux_specs=(res_spec,), tm=tm, tn=tn)

        hf = _rmsnorm(xf, norm_ffn[i], BF16)
        act = _matmul(hf, ffn_w13, i, [ident, lambda j, o=ffn_hidden // tn_glu: o + j], ffn_hidden,
                      _epi_swiglu, BF16, tm=tm, tn=tn_glu)
        xf = _matmul_ksplit_residual(act, ffn_w2[i].astype(BF16), xf, tm=tm, tn=tn,
                                     tk=ffn_hidden // 4)

    out = _rmsnorm(xf, norm_final, F32)
    return out.reshape(batch, seq, d)
```

```python
import functools
import math

import jax
import jax.numpy as jnp
from jax import lax
from jax.experimental import pallas as pl
from jax.experimental.pallas import tpu as pltpu

F32 = jnp.float32
BF16 = jnp.bfloat16

DIFF_HEADS = 8
DIFF_HEAD_DIM = 64
RET_HEADS = 8
RET_HEAD_DIM = 128
HEAD_W = 128
CONV_WIDTH = 31
N_BRANCH = 3
XATTN_HEADS = 4
ROPE_THETA = 10000.0
EPS = 1e-6
NEG_INF = -1e30

V7X_VMEM_BYTES = 64 * 1024 * 1024
VMEM_LIMIT_BYTES = V7X_VMEM_BYTES - 12 * 1024 * 1024
LANES = 128
SUBLANES = 8


def _params(semantics):
    return pltpu.CompilerParams(dimension_semantics=semantics, vmem_limit_bytes=VMEM_LIMIT_BYTES)


def _rmsnorm_kernel(x_ref, g_ref, o_ref):
    x = x_ref[...]
    y = x * lax.rsqrt(jnp.mean(x * x, axis=-1, keepdims=True) + EPS)
    o_ref[...] = (y * g_ref[...]).astype(o_ref.dtype)


def _rmsnorm(x, g, out_dtype, tm=512):
    m, d = x.shape
    return pl.pallas_call(
        _rmsnorm_kernel,
        out_shape=jax.ShapeDtypeStruct((m, d), out_dtype),
        grid=(m // tm,),
        in_specs=[pl.BlockSpec((tm, d), lambda i: (i, 0)),
                  pl.BlockSpec((1, d), lambda i: (0, 0))],
        out_specs=pl.BlockSpec((tm, d), lambda i: (i, 0)),
        compiler_params=_params(("parallel",)),
        name="rmsnorm",
    )(x, g.reshape(1, d))


def _inv_rms_lanes(sum_sq, d):
    return jnp.broadcast_to(lax.rsqrt(sum_sq * (1.0 / d) + EPS), (sum_sq.shape[0], LANES))


def _lane_block_sums(sq):
    return functools.reduce(jnp.add, [sq[:, c:c + LANES] for c in range(0, sq.shape[1], LANES)])


def _prep_norm_kernel(x_ref, g_ref, xb_ref, inv_ref):
    x = x_ref[...]
    xb_ref[...] = (x * g_ref[...]).astype(xb_ref.dtype)
    inv_ref[...] = _inv_rms_lanes(jnp.sum(x * x, axis=-1, keepdims=True), x.shape[1])


def _prep_norm(x, g, tm=512):
    m, d = x.shape
    return pl.pallas_call(
        _prep_norm_kernel,
        out_shape=(jax.ShapeDtypeStruct((m, d), BF16), jax.ShapeDtypeStruct((m, LANES), F32)),
        grid=(m // tm,),
        in_specs=[pl.BlockSpec((tm, d), lambda i: (i, 0)),
                  pl.BlockSpec((1, d), lambda i: (0, 0))],
        out_specs=(pl.BlockSpec((tm, d), lambda i: (i, 0)), pl.BlockSpec((tm, LANES), lambda i: (i, 0))),
        compiler_params=_params(("parallel",)),
        name="prep_norm",
    )(x, g.reshape(1, d))


def _finish_inv_kernel(p_ref, inv_ref, *, d):
    inv_ref[...] = _inv_rms_lanes(jnp.sum(p_ref[...], axis=-1, keepdims=True), d)


def _finish_inv(partials, d, tm=2048):
    m, w = partials.shape
    return pl.pallas_call(
        functools.partial(_finish_inv_kernel, d=d),
        out_shape=jax.ShapeDtypeStruct((m, LANES), F32),
        grid=(m // tm,),
        in_specs=[pl.BlockSpec((tm, w), lambda i: (i, 0))],
        out_specs=pl.BlockSpec((tm, LANES), lambda i: (i, 0)),
        compiler_params=_params(("parallel",)),
        name="finish_inv",
    )(partials)


def _epi_cast(accs, aux):
    return accs[0]


def _sigmoid(x):
    return 0.5 * jnp.tanh(0.5 * x) + 0.5


def _epi_sigmoid(accs, aux):
    return _sigmoid(accs[0])


def _epi_silu(accs, aux):
    return accs[0] * _sigmoid(accs[0])


def _epi_glu(accs, aux):
    return accs[0] * _sigmoid(accs[1])


def _epi_swiglu(accs, aux):
    return accs[0] * _sigmoid(accs[0]) * accs[1]


def _epi_rope(accs, aux, *, half):
    acc = accs[0]
    cos, sin_signed = aux[0][...], aux[1][...]
    tm, tn = acc.shape
    lane = lax.broadcasted_iota(jnp.int32, (tm, LANES), 1)
    first_half = (lane % (2 * half)) < half
    outs = []
    for c in range(tn // LANES):
        x = acc[:, c * LANES:(c + 1) * LANES]
        if 2 * half == LANES:
            partner = pltpu.roll(x, half, 1)
        else:
            partner = jnp.where(first_half, pltpu.roll(x, LANES - half, 1), pltpu.roll(x, half, 1))
        outs.append(x * cos + partner * sin_signed)
    return jnp.concatenate(outs, axis=1)


def _cast_weights_once(w_refs, wb_refs):
    @pl.when(pl.program_id(1) == 0)
    def _():
        for w, wb in zip(w_refs, wb_refs):
            wb[...] = w[0].astype(BF16)


def _mm_kernel(*refs, n_w, epilogue, row_scaled):
    a_ref = refs[0]
    w_refs = refs[1:1 + n_w]
    aux_refs = refs[1 + n_w:-1 - n_w]
    o_ref = refs[-1 - n_w]
    wb_refs = refs[-n_w:]
    _cast_weights_once(w_refs, wb_refs)
    a = a_ref[...]
    accs = [jnp.dot(a, wb[...], preferred_element_type=F32) for wb in wb_refs]
    if row_scaled:
        inv = jnp.tile(aux_refs[0][...], (1, o_ref.shape[1] // LANES))
        accs = [acc * inv for acc in accs]
        aux_refs = aux_refs[1:]
    o_ref[...] = epilogue(accs, aux_refs).astype(o_ref.dtype)


def _matmul(a, w, layer, col_blocks, n_out, epilogue, out_dtype, inv=None, aux=(), aux_specs=(),
            tm=1024, tn=1024):
    m, k = a.shape
    n_w = len(col_blocks)
    in_specs = [pl.BlockSpec((tm, k), lambda j, i: (i, 0))]
    for cb in col_blocks:
        in_specs.append(pl.BlockSpec((1, k, tn), lambda j, i, cb=cb: (layer, 0, cb(j))))
    if inv is not None:
        in_specs.append(pl.BlockSpec((tm, LANES), lambda j, i: (i, 0)))
        aux = (inv,) + tuple(aux)
    in_specs.extend(aux_specs)
    return pl.pallas_call(
        functools.partial(_mm_kernel, n_w=n_w, epilogue=epilogue, row_scaled=inv is not None),
        out_shape=jax.ShapeDtypeStruct((m, n_out), out_dtype),
        grid=(n_out // tn, m // tm),
        in_specs=in_specs,
        out_specs=pl.BlockSpec((tm, tn), lambda j, i: (i, j)),
        scratch_shapes=[pltpu.VMEM((k, tn), BF16) for _ in range(n_w)],
        compiler_params=_params(("arbitrary", "arbitrary")),
        name="matmul_" + getattr(epilogue, "__name__", "rope"),
    )(a, *([w] * n_w), *aux)


def _emit_residual(x, g_ref, out_refs):
    out_refs[0][...] = x
    if len(out_refs) > 1:
        out_refs[1][...] = (x * g_ref[...]).astype(BF16)
        out_refs[2][...] = _lane_block_sums(x * x)


def _residual_out(m, n, tm, tn, emit_norm, row_col):
    shapes = [jax.ShapeDtypeStruct((m, n), F32)]
    specs = [pl.BlockSpec((tm, tn), row_col)]
    if emit_norm:
        shapes += [jax.ShapeDtypeStruct((m, n), BF16), jax.ShapeDtypeStruct((m, (n // tn) * LANES), F32)]
        specs += [pl.BlockSpec((tm, tn), row_col), pl.BlockSpec((tm, LANES), row_col)]
    return tuple(shapes), tuple(specs)


def _mm_residual_kernel(a_ref, w_ref, r_ref, g_ref, *refs):
    out_refs, wb_ref = refs[:-1], refs[-1]
    _cast_weights_once([w_ref], [wb_ref])
    x = r_ref[...] + jnp.dot(a_ref[...], wb_ref[...], preferred_element_type=F32)
    _emit_residual(x, g_ref, out_refs)


def _matmul_residual(a, w, layer, res, g_next, tm=1024, tn=1024):
    m, k = a.shape
    n = res.shape[1]
    shapes, specs = _residual_out(m, n, tm, tn, True, lambda j, i: (i, j))
    return pl.pallas_call(
        _mm_residual_kernel,
        out_shape=shapes,
        grid=(n // tn, m // tm),
        in_specs=[pl.BlockSpec((tm, k), lambda j, i: (i, 0)),
                  pl.BlockSpec((1, k, tn), lambda j, i: (layer, 0, j), pipeline_mode=pl.Buffered(1)),
                  pl.BlockSpec((tm, tn), lambda j, i: (i, j)),
                  pl.BlockSpec((1, tn), lambda j, i: (0, j))],
        out_specs=specs,
        scratch_shapes=[pltpu.VMEM((k, tn), BF16)],
        compiler_params=_params(("arbitrary", "arbitrary")),
        name="matmul_residual",
    )(a, w, res, g_next.reshape(1, n))


def _mm_rows_kernel(a_ref, w_ref, r_ref, g_ref, *out_refs):
    x = r_ref[...] + jnp.dot(a_ref[...], w_ref[...], preferred_element_type=F32)
    _emit_residual(x, g_ref, out_refs)


def _matmul_rows_residual(a, w, res, g_next, emit_norm, tm=1024, tn=512):
    m, k = a.shape
    n = w.shape[1]
    shapes, specs = _residual_out(m, n, tm, tn, emit_norm, lambda i, j: (i, j))
    return pl.pallas_call(
        _mm_rows_kernel,
        out_shape=shapes,
        grid=(m // tm, n // tn),
        in_specs=[pl.BlockSpec((tm, k), lambda i, j: (i, 0)),
                  pl.BlockSpec((k, tn), lambda i, j: (0, j)),
                  pl.BlockSpec((tm, tn), lambda i, j: (i, j)),
                  pl.BlockSpec((1, tn), lambda i, j: (0, j))],
        out_specs=specs,
        compiler_params=_params(("parallel", "arbitrary")),
        name="matmul_rows_residual",
    )(a, w, res, g_next.reshape(1, n))


def _diff_attn_kernel(lam_ref, sub_ref, q_ref, k_ref, v_ref, o_ref, vt_ref, acc_ref, *, tq, group,
                      lambda_init):
    qi = pl.program_id(2)
    tk = tq
    nkv = v_ref.shape[0] // tk
    heads = [slice(g * HEAD_W, (g + 1) * HEAD_W) for g in range(group)]

    @pl.when(qi == 0)
    def _():
        for g in range(group):
            for jb in range(nkv):
                vt_ref[g, jb] = v_ref[jb * tk:(jb + 1) * tk, heads[g]].astype(F32).T.astype(BF16)

    scale = DIFF_HEAD_DIM ** -0.5 * math.log2(math.e)
    dim = lax.broadcasted_iota(jnp.int32, (HEAD_W, tq), 0)
    qsts = []
    for g in range(group):
        qt = (q_ref[:, heads[g]].astype(F32) * scale).T
        qsts.append(jnp.concatenate([jnp.where(dim < DIFF_HEAD_DIM, qt, 0.0),
                                     jnp.where(dim >= DIFF_HEAD_DIM, qt, 0.0)], axis=1).astype(BF16))

    def block(j, carry, masked):
        new = []
        logits = [jnp.dot(k_ref[pl.ds(pl.multiple_of(j * tk, tk), tk), heads[g]], qsts[g],
                          preferred_element_type=F32) for g in range(group)]
        for g in range(group):
            m_prev, l_prev = carry[g]
            s = logits[g]
            if masked:
                key = lax.broadcasted_iota(jnp.int32, (tk, 2 * tq), 0)
                qry = lax.broadcasted_iota(jnp.int32, (tk, 2 * tq), 1)
                qry = jnp.where(qry >= tq, qry - tq, qry)
                s = jnp.where(key <= qry, s, NEG_INF)
            m_new = jnp.maximum(m_prev, jnp.max(s, axis=0, keepdims=True))
            alpha = jnp.exp2(m_prev - m_new)
            p = jnp.exp2(s - m_new)
            l_new = alpha * l_prev + jnp.sum(p, axis=0, keepdims=True)
            acc_ref[g] = alpha * acc_ref[g] + jnp.dot(vt_ref[g, j], p.astype(BF16),
                                                      preferred_element_type=F32)
            new.append((m_new, l_new))
        return tuple(new)

    acc_ref[...] = jnp.zeros_like(acc_ref)
    init = tuple((jnp.full((1, 2 * tq), NEG_INF, F32), jnp.zeros((1, 2 * tq), F32)) for _ in range(group))
    carry = lax.fori_loop(0, qi, functools.partial(block, masked=False), init)
    carry = block(qi, carry, masked=True)

    lam_p = lam_ref[...]
    lam = (jnp.exp(jnp.sum(lam_p[0:1] * lam_p[1:2], axis=-1, keepdims=True))
           - jnp.exp(jnp.sum(lam_p[2:3] * lam_p[3:4], axis=-1, keepdims=True)) + lambda_init)
    for g in range(group):
        o = acc_ref[g] * (1.0 / carry[g][1])
        y = o[:, :tq] - lam * o[:, tq:]
        y = y * lax.rsqrt(jnp.mean(y * y, axis=0, keepdims=True) + EPS) * sub_ref[...]
        o_ref[:, heads[g]] = (y * (1.0 - lambda_init)).T.astype(o_ref.dtype)


def _diff_attention(qk, v, diff_lambda, diff_subln, batch, seq, lambda_init, tq=256, group=4):
    nq = seq // tq
    n_groups = DIFF_HEADS // group
    gw = group * HEAD_W
    return pl.pallas_call(
        functools.partial(_diff_attn_kernel, tq=tq, group=group, lambda_init=lambda_init),
        out_shape=jax.ShapeDtypeStruct((batch * seq, DIFF_HEADS * HEAD_W), BF16),
        grid=(batch, n_groups, nq),
        in_specs=[pl.BlockSpec((4, DIFF_HEAD_DIM), lambda b, h, i: (0, 0)),
                  pl.BlockSpec((HEAD_W, 1), lambda b, h, i: (0, 0)),
                  pl.BlockSpec((tq, gw), lambda b, h, i: (b * nq + i, h)),
                  pl.BlockSpec((seq, gw), lambda b, h, i: (b, n_groups + h)),
                  pl.BlockSpec((seq, gw), lambda b, h, i: (b, h))],
        out_specs=pl.BlockSpec((tq, gw), lambda b, h, i: (b * nq + i, h)),
        scratch_shapes=[pltpu.VMEM((group, nq, HEAD_W, tq), BF16),
                        pltpu.VMEM((group, HEAD_W, 2 * tq), F32)],
        compiler_params=_params(("parallel", "parallel", "arbitrary")),
        name="diff_attention",
    )(diff_lambda, diff_subln.reshape(HEAD_W, 1), qk, qk, v)


CONV_HALO = 32
LN_ROWS = 64


def _conv_kernel(w_ref, b_ref, g_ref, beta_ref, prev_ref, cur_ref, o_ref, sh_ref, y_ref, *, tt, rows,
                 lanes):
    i = pl.program_id(1)
    ch = sh_ref.shape[2]
    n_ext = CONV_HALO + tt

    @pl.when(i == 0)
    def _():
        sh_ref[0, 0:CONV_HALO, :] = jnp.zeros((CONV_HALO, ch), F32)

    @pl.when(i > 0)
    def _():
        sh_ref[0, 0:CONV_HALO, :] = prev_ref[...]

    sh_ref[0, CONV_HALO:n_ext, :] = cur_ref[...]
    for c0 in range(0, ch, LANES):
        ext = sh_ref[0, :, c0:c0 + LANES]
        for s in range(1, SUBLANES):
            sh_ref[s, 0:n_ext - SUBLANES, c0:c0 + LANES] = pltpu.roll(ext, n_ext - s, 0)[0:n_ext - SUBLANES]

    first = CONV_HALO - (CONV_WIDTH - 1)
    groups = rows // SUBLANES

    def row_chunk(c, carry):
        r0 = pl.multiple_of(c * rows, rows)
        for c0 in range(0, ch, lanes):
            acc = jnp.zeros((groups, SUBLANES, lanes), F32)
            for s in range(SUBLANES):
                offs = [o for o in range(first, first + CONV_WIDTH) if o % SUBLANES == s]
                a_lo, a_hi = offs[0] // SUBLANES, offs[-1] // SUBLANES
                span = sh_ref[s, pl.ds(r0 + SUBLANES * a_lo, rows + SUBLANES * (a_hi - a_lo)), c0:c0 + lanes]
                span = span.reshape(groups + a_hi - a_lo, SUBLANES, lanes)
                for o in offs:
                    a = o // SUBLANES - a_lo
                    acc = acc + w_ref[o - first, :, c0:c0 + lanes] * span[a:a + groups]
            y_ref[pl.ds(r0, rows), c0:c0 + lanes] = acc.reshape(rows, lanes)
        return carry

    lax.fori_loop(0, tt // rows, row_chunk, 0)

    for r0 in range(0, tt, LN_ROWS):
        y = y_ref[r0:r0 + LN_ROWS, :] + b_ref[...]
        mu = jnp.mean(y, axis=-1, keepdims=True)
        var = jnp.mean(jnp.square(y - mu), axis=-1, keepdims=True)
        y = (y - mu) * lax.rsqrt(var + EPS) * g_ref[...] + beta_ref[...]
        o_ref[r0:r0 + LN_ROWS, :] = (y * _sigmoid(y)).astype(o_ref.dtype)


def _conv_module(u, conv_w, conv_b, ln_g, ln_b, batch, seq, tt=256, rows=32, lanes=256):
    ch = u.shape[1]
    conv_w = jnp.broadcast_to(conv_w[:, None, :], (CONV_WIDTH, SUBLANES, ch))
    nt = seq // tt
    halo_per_tile = tt // CONV_HALO
    vec = lambda a: a.reshape(1, ch)
    row_spec = pl.BlockSpec((1, ch), lambda b, i: (0, 0))
    return pl.pallas_call(
        functools.partial(_conv_kernel, tt=tt, rows=rows, lanes=lanes),
        out_shape=jax.ShapeDtypeStruct((batch * seq, ch), BF16),
        grid=(batch, nt),
        in_specs=[pl.BlockSpec((CONV_WIDTH, SUBLANES, ch), lambda b, i: (0, 0, 0)),
                  row_spec, row_spec, row_spec,
                  pl.BlockSpec((CONV_HALO, ch),
                               lambda b, i: (jnp.maximum((b * nt + i) * halo_per_tile - 1, 0), 0)),
                  pl.BlockSpec((tt, ch), lambda b, i: (b * nt + i, 0))],
        out_specs=pl.BlockSpec((tt, ch), lambda b, i: (b * nt + i, 0)),
        scratch_shapes=[pltpu.VMEM((SUBLANES, CONV_HALO + tt, ch), F32), pltpu.VMEM((tt, ch), F32)],
        compiler_params=_params(("parallel", "arbitrary")),
        name="conv_module",
    )(conv_w, vec(conv_b), vec(ln_g), vec(ln_b), u, u)


def _retention_kernel(lg_ref, g_ref, q_ref, k_ref, v_ref, gate_ref, o_ref, state_ref, *, chunk, seq):
    lg = lg_ref[0][:, 0:1]
    scale = RET_HEAD_DIM ** -0.5
    row = lax.broadcasted_iota(jnp.int32, (chunk, chunk), 0)
    col = lax.broadcasted_iota(jnp.int32, (chunk, chunk), 1)
    rel = (row - col).astype(F32)
    decay = jnp.where(rel >= 0, jnp.exp(lg * jnp.maximum(rel, 0.0)), 0.0) * scale
    idx = lax.broadcasted_iota(jnp.int32, (chunk, 1), 0).astype(F32)
    q_decay = jnp.exp(lg * (idx + 1.0)) * scale
    k_decay = jnp.exp(lg * (chunk - 1.0 - idx))
    chunk_decay = jnp.exp(lg * chunk)
    state_ref[...] = jnp.zeros_like(state_ref)
    for n in range(seq // chunk):
        sl = slice(n * chunk, (n + 1) * chunk)
        q = q_ref[sl, :]
        k = k_ref[sl, :]
        v = v_ref[sl, :]
        scores = lax.dot_general(q, k, (((1,), (1,)), ((), ())), preferred_element_type=F32) * decay
        y = jnp.dot(scores.astype(BF16), v, preferred_element_type=F32)
        state = state_ref[...]
        y = y + jnp.dot((q.astype(F32) * q_decay).astype(BF16), state.astype(BF16),
                        preferred_element_type=F32)
        kd_t = (k.astype(F32) * k_decay).T.astype(BF16)
        state_ref[...] = state * chunk_decay + jnp.dot(kd_t, v, preferred_element_type=F32)
        mu = jnp.mean(y, axis=-1, keepdims=True)
        var = jnp.mean(jnp.square(y - mu), axis=-1, keepdims=True)
        y = (y - mu) * lax.rsqrt(var + EPS) * g_ref[...]
        o_ref[sl, :] = (gate_ref[sl, :] * y).astype(o_ref.dtype)


def _retention(qk, v, gate, gn_g, batch, seq, chunk=256):
    log_gamma = jnp.log1p(-jnp.exp2(-5.0 - jnp.arange(RET_HEADS, dtype=F32)))
    lg = jnp.broadcast_to(log_gamma[:, None, None], (RET_HEADS, 1, LANES))
    head = lambda b, h: (b, h)
    return pl.pallas_call(
        functools.partial(_retention_kernel, chunk=chunk, seq=seq),
        out_shape=jax.ShapeDtypeStruct((batch * seq, RET_HEADS * HEAD_W), BF16),
        grid=(batch, RET_HEADS),
        in_specs=[pl.BlockSpec((1, 1, LANES), lambda b, h: (h, 0, 0)),
                  pl.BlockSpec((1, HEAD_W), lambda b, h: (0, h)),
                  pl.BlockSpec((seq, HEAD_W), head),
                  pl.BlockSpec((seq, HEAD_W), lambda b, h: (b, RET_HEADS + h)),
                  pl.BlockSpec((seq, HEAD_W), head),
                  pl.BlockSpec((seq, HEAD_W), head)],
        out_specs=pl.BlockSpec((seq, HEAD_W), head),
        scratch_shapes=[pltpu.VMEM((RET_HEAD_DIM, RET_HEAD_DIM), F32)],
        compiler_params=_params(("parallel", "parallel")),
        name="retention",
    )(lg, gn_g.reshape(1, -1), qk, qk, v, gate)


def _merge_kernel(ya_ref, yb_ref, yr_ref, wa_ref, wb_ref, wr_ref, ga_ref, gb_ref, gr_ref, o_ref):
    out = ga_ref[...].astype(F32) * jnp.dot(ya_ref[...], wa_ref[0], preferred_element_type=F32)
    out = out + gb_ref[...].astype(F32) * jnp.dot(yb_ref[...], wb_ref[0], preferred_element_type=F32)
    out = out + gr_ref[...].astype(F32) * jnp.dot(yr_ref[...], wr_ref[0], preferred_element_type=F32)
    o_ref[...] = out.astype(o_ref.dtype)


def _merge(ya, yb, yr, w_branch, gates, tm=1024, tn=1024):
    m, kdim = ya.shape
    d = w_branch.shape[2]
    nb = d // tn
    y_spec = pl.BlockSpec((tm, kdim), lambda i, j: (i, 0))
    w_spec = lambda b: pl.BlockSpec((1, kdim, tn), lambda i, j: (b, 0, j))
    g_spec = lambda b: pl.BlockSpec((tm, tn), lambda i, j: (i, b * nb + j))
    return pl.pallas_call(
        _merge_kernel,
        out_shape=jax.ShapeDtypeStruct((m, d), BF16),
        grid=(m // tm, nb),
        in_specs=[y_spec, y_spec, y_spec, w_spec(0), w_spec(1), w_spec(2),
                  g_spec(0), g_spec(1), g_spec(2)],
        out_specs=pl.BlockSpec((tm, tn), lambda i, j: (i, j)),
        compiler_params=_params(("parallel", "arbitrary")),
        name="branch_merge",
    )(ya, yb, yr, w_branch, w_branch, w_branch, gates, gates, gates)


def _xattn_kernel(q_ref, k_ref, v_ref, o_ref, *, scale, hd):
    heads = [slice(h * hd, (h + 1) * hd) for h in range(XATTN_HEADS)]
    logits = [lax.dot_general(q_ref[:, hs], k_ref[:, hs], (((1,), (1,)), ((), ())),
                              preferred_element_type=F32) for hs in heads]
    for hs, s in zip(heads, logits):
        s = s * scale
        p = jnp.exp(s - jnp.max(s, axis=-1, keepdims=True))
        l = jnp.sum(p, axis=-1, keepdims=True)
        o = jnp.dot(p.astype(BF16), v_ref[:, hs], preferred_element_type=F32) / l
        o_ref[:, hs] = o.astype(o_ref.dtype)


def _cross_attention(q, kv, batch, seq, mem_tokens, tq=512):
    d = q.shape[1]
    hd = d // XATTN_HEADS
    nq = seq // tq
    return pl.pallas_call(
        functools.partial(_xattn_kernel, scale=hd ** -0.5, hd=hd),
        out_shape=jax.ShapeDtypeStruct((batch * seq, d), BF16),
        grid=(batch, nq),
        in_specs=[pl.BlockSpec((tq, d), lambda b, i: (b * nq + i, 0)),
                  pl.BlockSpec((mem_tokens, d), lambda b, i: (b, 0)),
                  pl.BlockSpec((mem_tokens, d), lambda b, i: (b, 1))],
        out_specs=pl.BlockSpec((tq, d), lambda b, i: (b * nq + i, 0)),
        compiler_params=_params(("parallel", "arbitrary")),
        name="cross_attention",
    )(q, kv, kv)


def _rope_tables(seq, half, inv_freq):
    ang = jnp.arange(seq, dtype=F32)[:, None] * inv_freq[None, :]
    cos, sin = jnp.cos(ang), jnp.sin(ang)
    reps = LANES // (2 * half)
    cos_l = jnp.tile(jnp.concatenate([cos, cos], axis=1), (1, reps))
    sin_l = jnp.tile(jnp.concatenate([-sin, sin], axis=1), (1, reps))
    return cos_l, sin_l


def kernel(x, mem, norm_mix, w_in, diff_lambda, diff_subln, conv_w, conv_b, conv_ln_g, conv_ln_b,
           ret_gn_g, w_branch, w_out, norm_xattn, norm_mem, xattn_wq, xattn_wkv, xattn_wo,
           norm_ffn, ffn_w13, ffn_w2, norm_final):
    batch, seq, d = x.shape
    mem_tokens = mem.shape[1]
    depth = w_in.shape[0]
    mix_w = d // 2
    ffn_hidden = ffn_w2.shape[1]
    tm = 1024
    tn = 1024
    rows_per_seq = seq // tm

    diff_inv = ROPE_THETA ** (-jnp.arange(0, DIFF_HEAD_DIM, 2, dtype=F32) / DIFF_HEAD_DIM)
    diff_tabs = _rope_tables(seq, DIFF_HEAD_DIM // 2, diff_inv)
    ret_inv = 1.0 / (ROPE_THETA ** jnp.linspace(0.0, 1.0, RET_HEAD_DIM // 2, dtype=F32))
    ret_tabs = _rope_tables(seq, RET_HEAD_DIM // 2, ret_inv)
    tab_spec = pl.BlockSpec((tm, LANES), lambda j, i: (i % rows_per_seq, 0))
    tn_glu = 512

    xf = x.reshape(batch * seq, d)
    memf = mem.reshape(batch * mem_tokens, d)
    seg = {"dq": 0, "dv": 2 * mix_w, "ca": 3 * mix_w, "cb": 4 * mix_w, "rq": 5 * mix_w, "rv": 7 * mix_w,
           "rg": 8 * mix_w, "gates": 9 * mix_w}

    def off(name, width=tn):
        return lambda j, o=seg[name] // width: o + j

    ident = lambda j: j

    xb, inv = _prep_norm(xf, norm_mix[0])

    for i in range(depth):
        lambda_init = 0.8 - 0.6 * math.exp(-0.3 * i)
        last = i == depth - 1

        dqk = _matmul(xb, w_in, i, [off("dq")], 2 * mix_w,
                      functools.partial(_epi_rope, half=DIFF_HEAD_DIM // 2), BF16, inv=inv,
                      aux=diff_tabs, aux_specs=(tab_spec, tab_spec), tm=tm, tn=tn)
        rqk = _matmul(xb, w_in, i, [off("rq")], 2 * mix_w,
                      functools.partial(_epi_rope, half=RET_HEAD_DIM // 2), BF16, inv=inv,
                      aux=ret_tabs, aux_specs=(tab_spec, tab_spec), tm=tm, tn=tn)
        dv = _matmul(xb, w_in, i, [off("dv")], mix_w, _epi_cast, BF16, inv=inv, tm=tm, tn=tn)
        rv = _matmul(xb, w_in, i, [off("rv")], mix_w, _epi_cast, BF16, inv=inv, tm=tm, tn=tn)
        u = _matmul(xb, w_in, i, [off("ca", tn_glu), off("cb", tn_glu)], mix_w, _epi_glu, F32, inv=inv,
                    tm=tm, tn=tn_glu)
        rgate = _matmul(xb, w_in, i, [off("rg")], mix_w, _epi_silu, F32, inv=inv, tm=tm, tn=tn)
        gates = _matmul(xb, w_in, i, [off("gates")], N_BRANCH * d, _epi_sigmoid, BF16, inv=inv,
                        tm=tm, tn=tn)

        ya = _diff_attention(dqk, dv, diff_lambda[i], diff_subln[i], batch, seq, lambda_init)
        yb = _conv_module(u, conv_w[i], conv_b[i], conv_ln_g[i], conv_ln_b[i], batch, seq)
        yr = _retention(rqk, rv, rgate, ret_gn_g[i], batch, seq)

        merged = _merge(ya, yb, yr, w_branch[i].astype(BF16), gates, tm=tm, tn=tn)
        xf, xb, ssq = _matmul_residual(merged, w_out, i, xf, norm_xattn[i], tm=tm, tn=tn)
        inv = _finish_inv(ssq, d)

        hm = _rmsnorm(memf, norm_mem[i], BF16, tm=256)
        q = _matmul(xb, xattn_wq, i, [ident], d, _epi_cast, BF16, inv=inv, tm=tm, tn=tn)
        kv = _matmul(hm, xattn_wkv, i, [ident], 2 * d, _epi_cast, BF16,
                     tm=min(tm, batch * mem_tokens), tn=tn)
        o = _cross_attention(q, kv, batch, seq, mem_tokens)
        xf, xb, ssq = _matmul_residual(o, xattn_wo, i, xf, norm_ffn[i], tm=tm, tn=tn)
        inv = _finish_inv(ssq, d)

        act = _matmul(xb, ffn_w13, i, [ident, lambda j, o=ffn_hidden // tn_glu: o + j], ffn_hidden,
                      _epi_swiglu, BF16, inv=inv, tm=tm, tn=tn_glu)
        g_next = norm_final if last else norm_mix[i + 1]
        outs = _matmul_rows_residual(act, ffn_w2[i].astype(BF16), xf, g_next, not last, tm=tm, tn=tn_glu)
        if last:
            xf = outs[0]
        else:
            xf, xb, ssq = outs
            inv = _finish_inv(ssq, d)

    out = _rmsnorm(xf, norm_final, F32)
    return out.reshape(batch, seq, d)
```

```python
import functools
import math

import jax
import jax.numpy as jnp
from jax import lax
from jax.experimental import pallas as pl
from jax.experimental.pallas import tpu as pltpu

F32 = jnp.float32
BF16 = jnp.bfloat16

DIFF_HEADS = 8
DIFF_HEAD_DIM = 64
RET_HEADS = 8
RET_HEAD_DIM = 128
HEAD_W = 128
CONV_WIDTH = 31
N_BRANCH = 3
XATTN_HEADS = 4
ROPE_THETA = 10000.0
EPS = 1e-6
NEG_INF = -1e30

V7X_VMEM_BYTES = 64 * 1024 * 1024
VMEM_LIMIT_BYTES = V7X_VMEM_BYTES - 12 * 1024 * 1024
LANES = 128
SUBLANES = 8


def _params(semantics):
    return pltpu.CompilerParams(dimension_semantics=semantics, vmem_limit_bytes=VMEM_LIMIT_BYTES)


def _rmsnorm_kernel(x_ref, g_ref, o_ref):
    x = x_ref[...]
    y = x * lax.rsqrt(jnp.mean(x * x, axis=-1, keepdims=True) + EPS)
    o_ref[...] = (y * g_ref[...]).astype(o_ref.dtype)


def _rmsnorm(x, g, out_dtype, tm=512):
    m, d = x.shape
    return pl.pallas_call(
        _rmsnorm_kernel,
        out_shape=jax.ShapeDtypeStruct((m, d), out_dtype),
        grid=(m // tm,),
        in_specs=[pl.BlockSpec((tm, d), lambda i: (i, 0)),
                  pl.BlockSpec((1, d), lambda i: (0, 0))],
        out_specs=pl.BlockSpec((tm, d), lambda i: (i, 0)),
        compiler_params=_params(("parallel",)),
        name="rmsnorm",
    )(x, g.reshape(1, d))


def _inv_rms_lanes(sum_sq, d):
    return jnp.broadcast_to(lax.rsqrt(sum_sq * (1.0 / d) + EPS), (sum_sq.shape[0], LANES))


def _lane_block_sums(sq):
    return functools.reduce(jnp.add, [sq[:, c:c + LANES] for c in range(0, sq.shape[1], LANES)])


def _prep_norm_kernel(x_ref, g_ref, xb_ref, inv_ref):
    x = x_ref[...]
    xb_ref[...] = (x * g_ref[...]).astype(xb_ref.dtype)
    inv_ref[...] = _inv_rms_lanes(jnp.sum(x * x, axis=-1, keepdims=True), x.shape[1])


def _prep_norm(x, g, tm=512):
    m, d = x.shape
    return pl.pallas_call(
        _prep_norm_kernel,
        out_shape=(jax.ShapeDtypeStruct((m, d), BF16), jax.ShapeDtypeStruct((m, LANES), F32)),
        grid=(m // tm,),
        in_specs=[pl.BlockSpec((tm, d), lambda i: (i, 0)),
                  pl.BlockSpec((1, d), lambda i: (0, 0))],
        out_specs=(pl.BlockSpec((tm, d), lambda i: (i, 0)), pl.BlockSpec((tm, LANES), lambda i: (i, 0))),
        compiler_params=_params(("parallel",)),
        name="prep_norm",
    )(x, g.reshape(1, d))


def _finish_inv_kernel(p_ref, inv_ref, *, d):
    inv_ref[...] = _inv_rms_lanes(jnp.sum(p_ref[...], axis=-1, keepdims=True), d)


def _finish_inv(partials, d, tm=2048):
    m, w = partials.shape
    return pl.pallas_call(
        functools.partial(_finish_inv_kernel, d=d),
        out_shape=jax.ShapeDtypeStruct((m, LANES), F32),
        grid=(m // tm,),
        in_specs=[pl.BlockSpec((tm, w), lambda i: (i, 0))],
        out_specs=pl.BlockSpec((tm, LANES), lambda i: (i, 0)),
        compiler_params=_params(("parallel",)),
        name="finish_inv",
    )(partials)


def _epi_cast(accs, aux):
    return accs[0]


def _sigmoid(x):
    return 0.5 * jnp.tanh(0.5 * x) + 0.5


def _epi_sigmoid(accs, aux):
    return _sigmoid(accs[0])


def _epi_silu(accs, aux):
    return accs[0] * _sigmoid(accs[0])


def _epi_glu(accs, aux):
    return accs[0] * _sigmoid(accs[1])


def _epi_swiglu(accs, aux):
    return accs[0] * _sigmoid(accs[0]) * accs[1]


def _epi_rope(accs, aux, *, half):
    acc = accs[0]
    cos, sin_signed = aux[0][...], aux[1][...]
    tm, tn = acc.shape
    lane = lax.broadcasted_iota(jnp.int32, (tm, LANES), 1)
    first_half = (lane % (2 * half)) < half
    outs = []
    for c in range(tn // LANES):
        x = acc[:, c * LANES:(c + 1) * LANES]
        if 2 * half == LANES:
            partner = pltpu.roll(x, half, 1)
        else:
            partner = jnp.where(first_half, pltpu.roll(x, LANES - half, 1), pltpu.roll(x, half, 1))
        outs.append(x * cos + partner * sin_signed)
    return jnp.concatenate(outs, axis=1)


def _cast_weights_once(w_refs, wb_refs):
    @pl.when(pl.program_id(1) == 0)
    def _():
        for w, wb in zip(w_refs, wb_refs):
            wb[...] = w[0].astype(BF16)


def _mm_kernel(*refs, n_w, epilogue, row_scaled):
    a_ref = refs[0]
    w_refs = refs[1:1 + n_w]
    aux_refs = refs[1 + n_w:-1 - n_w]
    o_ref = refs[-1 - n_w]
    wb_refs = refs[-n_w:]
    _cast_weights_once(w_refs, wb_refs)
    a = a_ref[...]
    accs = [jnp.dot(a, wb[...], preferred_element_type=F32) for wb in wb_refs]
    if row_scaled:
        inv = jnp.tile(aux_refs[0][...], (1, o_ref.shape[1] // LANES))
        accs = [acc * inv for acc in accs]
        aux_refs = aux_refs[1:]
    o_ref[...] = epilogue(accs, aux_refs).astype(o_ref.dtype)


def _matmul(a, w, layer, col_blocks, n_out, epilogue, out_dtype, inv=None, aux=(), aux_specs=(),
            tm=1024, tn=1024):
    m, k = a.shape
    n_w = len(col_blocks)
    in_specs = [pl.BlockSpec((tm, k), lambda j, i: (i, 0))]
    for cb in col_blocks:
        in_specs.append(pl.BlockSpec((1, k, tn), lambda j, i, cb=cb: (layer, 0, cb(j))))
    if inv is not None:
        in_specs.append(pl.BlockSpec((tm, LANES), lambda j, i: (i, 0)))
        aux = (inv,) + tuple(aux)
    in_specs.extend(aux_specs)
    return pl.pallas_call(
        functools.partial(_mm_kernel, n_w=n_w, epilogue=epilogue, row_scaled=inv is not None),
        out_shape=jax.ShapeDtypeStruct((m, n_out), out_dtype),
        grid=(n_out // tn, m // tm),
        in_specs=in_specs,
        out_specs=pl.BlockSpec((tm, tn), lambda j, i: (i, j)),
        scratch_shapes=[pltpu.VMEM((k, tn), BF16) for _ in range(n_w)],
        compiler_params=_params(("arbitrary", "arbitrary")),
        name="matmul_" + getattr(epilogue, "__name__", "rope"),
    )(a, *([w] * n_w), *aux)


def _emit_residual(x, g_ref, out_refs):
    out_refs[0][...] = x
    if len(out_refs) > 1:
        out_refs[1][...] = (x * g_ref[...]).astype(BF16)
        out_refs[2][...] = _lane_block_sums(x * x)


def _residual_out(m, n, tm, tn, emit_norm, row_col):
    shapes = [jax.ShapeDtypeStruct((m, n), F32)]
    specs = [pl.BlockSpec((tm, tn), row_col)]
    if emit_norm:
        shapes += [jax.ShapeDtypeStruct((m, n), BF16), jax.ShapeDtypeStruct((m, (n // tn) * LANES), F32)]
        specs += [pl.BlockSpec((tm, tn), row_col), pl.BlockSpec((tm, LANES), row_col)]
    return tuple(shapes), tuple(specs)


def _mm_residual_kernel(a_ref, w_ref, r_ref, g_ref, *refs):
    out_refs, wb_ref = refs[:-1], refs[-1]
    _cast_weights_once([w_ref], [wb_ref])
    x = r_ref[...] + jnp.dot(a_ref[...], wb_ref[...], preferred_element_type=F32)
    _emit_residual(x, g_ref, out_refs)


def _matmul_residual(a, w, layer, res, g_next, tm=1024, tn=1024):
    m, k = a.shape
    n = res.shape[1]
    shapes, specs = _residual_out(m, n, tm, tn, True, lambda j, i: (i, j))
    return pl.pallas_call(
        _mm_residual_kernel,
        out_shape=shapes,
        grid=(n // tn, m // tm),
        in_specs=[pl.BlockSpec((tm, k), lambda j, i: (i, 0)),
                  pl.BlockSpec((1, k, tn), lambda j, i: (layer, 0, j), pipeline_mode=pl.Buffered(1)),
                  pl.BlockSpec((tm, tn), lambda j, i: (i, j)),
                  pl.BlockSpec((1, tn), lambda j, i: (0, j))],
        out_specs=specs,
        scratch_shapes=[pltpu.VMEM((k, tn), BF16)],
        compiler_params=_params(("arbitrary", "arbitrary")),
        name="matmul_residual",
    )(a, w, res, g_next.reshape(1, n))


def _mm_rows_kernel(a_ref, w_ref, r_ref, g_ref, *out_refs):
    x = r_ref[...] + jnp.dot(a_ref[...], w_ref[...], preferred_element_type=F32)
    _emit_residual(x, g_ref, out_refs)


def _matmul_rows_residual(a, w, res, g_next, emit_norm, tm=1024, tn=512):
    m, k = a.shape
    n = w.shape[1]
    shapes, specs = _residual_out(m, n, tm, tn, emit_norm, lambda i, j: (i, j))
    return pl.pallas_call(
        _mm_rows_kernel,
        out_shape=shapes,
        grid=(m // tm, n // tn),
        in_specs=[pl.BlockSpec((tm, k), lambda i, j: (i, 0)),
                  pl.BlockSpec((k, tn), lambda i, j: (0, j)),
                  pl.BlockSpec((tm, tn), lambda i, j: (i, j)),
                  pl.BlockSpec((1, tn), lambda i, j: (0, j))],
        out_specs=specs,
        compiler_params=_params(("parallel", "arbitrary")),
        name="matmul_rows_residual",
    )(a, w, res, g_next.reshape(1, n))


def _diff_attn_kernel(lam_ref, sub_ref, q_ref, k_ref, v_ref, o_ref, vt_ref, acc_ref, *, tq, group,
                      lambda_init):
    qi = pl.program_id(2)
    tk = tq
    nkv = v_ref.shape[0] // tk
    heads = [slice(g * HEAD_W, (g + 1) * HEAD_W) for g in range(group)]

    @pl.when(qi == 0)
    def _():
        for g in range(group):
            for jb in range(nkv):
                vt_ref[g, jb] = v_ref[jb * tk:(jb + 1) * tk, heads[g]].astype(F32).T.astype(BF16)

    scale = DIFF_HEAD_DIM ** -0.5 * math.log2(math.e)
    dim = lax.broadcasted_iota(jnp.int32, (HEAD_W, tq), 0)
    qsts = []
    for g in range(group):
        qt = (q_ref[:, heads[g]].astype(F32) * scale).T
        qsts.append(jnp.concatenate([jnp.where(dim < DIFF_HEAD_DIM, qt, 0.0),
                                     jnp.where(dim >= DIFF_HEAD_DIM, qt, 0.0)], axis=1).astype(BF16))

    def block(j, carry, masked):
        new = []
        logits = [jnp.dot(k_ref[pl.ds(pl.multiple_of(j * tk, tk), tk), heads[g]], qsts[g],
                          preferred_element_type=F32) for g in range(group)]
        for g in range(group):
            m_prev, l_prev = carry[g]
            s = logits[g]
            if masked:
                key = lax.broadcasted_iota(jnp.int32, (tk, 2 * tq), 0)
                qry = lax.broadcasted_iota(jnp.int32, (tk, 2 * tq), 1)
                qry = jnp.where(qry >= tq, qry - tq, qry)
                s = jnp.where(key <= qry, s, NEG_INF)
            m_new = jnp.maximum(m_prev, jnp.max(s, axis=0, keepdims=True))
            alpha = jnp.exp2(m_prev - m_new)
            p = jnp.exp2(s - m_new)
            l_new = alpha * l_prev + jnp.sum(p, axis=0, keepdims=True)
            acc_ref[g] = alpha * acc_ref[g] + jnp.dot(vt_ref[g, j], p.astype(BF16),
                                                      preferred_element_type=F32)
            new.append((m_new, l_new))
        return tuple(new)

    acc_ref[...] = jnp.zeros_like(acc_ref)
    init = tuple((jnp.full((1, 2 * tq), NEG_INF, F32), jnp.zeros((1, 2 * tq), F32)) for _ in range(group))
    carry = lax.fori_loop(0, qi, functools.partial(block, masked=False), init)
    carry = block(qi, carry, masked=True)

    lam_p = lam_ref[...]
    lam = (jnp.exp(jnp.sum(lam_p[0:1] * lam_p[1:2], axis=-1, keepdims=True))
           - jnp.exp(jnp.sum(lam_p[2:3] * lam_p[3:4], axis=-1, keepdims=True)) + lambda_init)
    for g in range(group):
        o = acc_ref[g] * (1.0 / carry[g][1])
        y = o[:, :tq] - lam * o[:, tq:]
        y = y * lax.rsqrt(jnp.mean(y * y, axis=0, keepdims=True) + EPS) * sub_ref[...]
        o_ref[:, heads[g]] = (y * (1.0 - lambda_init)).T.astype(o_ref.dtype)


def _diff_attention(qk, v, diff_lambda, diff_subln, batch, seq, lambda_init, tq=256, group=8):
    nq = seq // tq
    n_groups = DIFF_HEADS // group
    gw = group * HEAD_W
    return pl.pallas_call(
        functools.partial(_diff_attn_kernel, tq=tq, group=group, lambda_init=lambda_init),
        out_shape=jax.ShapeDtypeStruct((batch * seq, DIFF_HEADS * HEAD_W), BF16),
        grid=(batch, n_groups, nq),
        in_specs=[pl.BlockSpec((4, DIFF_HEAD_DIM), lambda b, h, i: (0, 0)),
                  pl.BlockSpec((HEAD_W, 1), lambda b, h, i: (0, 0)),
                  pl.BlockSpec((tq, gw), lambda b, h, i: (b * nq + i, h)),
                  pl.BlockSpec((seq, gw), lambda b, h, i: (b, n_groups + h)),
                  pl.BlockSpec((seq, gw), lambda b, h, i: (b, h))],
        out_specs=pl.BlockSpec((tq, gw), lambda b, h, i: (b * nq + i, h)),
        scratch_shapes=[pltpu.VMEM((group, nq, HEAD_W, tq), BF16),
                        pltpu.VMEM((group, HEAD_W, 2 * tq), F32)],
        compiler_params=_params(("parallel", "parallel", "arbitrary")),
        name="diff_attention",
    )(diff_lambda, diff_subln.reshape(HEAD_W, 1), qk, qk, v)


CONV_HALO = 32
LN_ROWS = 64


def _conv_kernel(w_ref, b_ref, g_ref, beta_ref, prev_ref, cur_ref, o_ref, sh_ref, y_ref, *, tt, rows,
                 lanes):
    i = pl.program_id(1)
    ch = sh_ref.shape[2]
    n_ext = CONV_HALO + tt

    @pl.when(i == 0)
    def _():
        sh_ref[0, 0:CONV_HALO, :] = jnp.zeros((CONV_HALO, ch), F32)

    @pl.when(i > 0)
    def _():
        sh_ref[0, 0:CONV_HALO, :] = prev_ref[...]

    sh_ref[0, CONV_HALO:n_ext, :] = cur_ref[...]
    for c0 in range(0, ch, LANES):
        ext = sh_ref[0, :, c0:c0 + LANES]
        for s in range(1, SUBLANES):
            sh_ref[s, 0:n_ext - SUBLANES, c0:c0 + LANES] = pltpu.roll(ext, n_ext - s, 0)[0:n_ext - SUBLANES]

    first = CONV_HALO - (CONV_WIDTH - 1)
    groups = rows // SUBLANES

    def row_chunk(c, carry):
        r0 = pl.multiple_of(c * rows, rows)
        for c0 in range(0, ch, lanes):
            acc = jnp.zeros((groups, SUBLANES, lanes), F32)
            for s in range(SUBLANES):
                offs = [o for o in range(first, first + CONV_WIDTH) if o % SUBLANES == s]
                a_lo, a_hi = offs[0] // SUBLANES, offs[-1] // SUBLANES
                span = sh_ref[s, pl.ds(r0 + SUBLANES * a_lo, rows + SUBLANES * (a_hi - a_lo)), c0:c0 + lanes]
                span = span.reshape(groups + a_hi - a_lo, SUBLANES, lanes)
                for o in offs:
                    a = o // SUBLANES - a_lo
                    acc = acc + w_ref[o - first, :, c0:c0 + lanes] * span[a:a + groups]
            y_ref[pl.ds(r0, rows), c0:c0 + lanes] = acc.reshape(rows, lanes)
        return carry

    lax.fori_loop(0, tt // rows, row_chunk, 0)

    for r0 in range(0, tt, LN_ROWS):
        y = y_ref[r0:r0 + LN_ROWS, :] + b_ref[...]
        mu = jnp.mean(y, axis=-1, keepdims=True)
        var = jnp.mean(jnp.square(y - mu), axis=-1, keepdims=True)
        y = (y - mu) * lax.rsqrt(var + EPS) * g_ref[...] + beta_ref[...]
        o_ref[r0:r0 + LN_ROWS, :] = (y * _sigmoid(y)).astype(o_ref.dtype)


def _conv_module(u, conv_w, conv_b, ln_g, ln_b, batch, seq, tt=256, rows=32, lanes=256):
    ch = u.shape[1]
    conv_w = jnp.broadcast_to(conv_w[:, None, :], (CONV_WIDTH, SUBLANES, ch))
    nt = seq // tt
    halo_per_tile = tt // CONV_HALO
    vec = lambda a: a.reshape(1, ch)
    row_spec = pl.BlockSpec((1, ch), lambda b, i: (0, 0))
    return pl.pallas_call(
        functools.partial(_conv_kernel, tt=tt, rows=rows, lanes=lanes),
        out_shape=jax.ShapeDtypeStruct((batch * seq, ch), BF16),
        grid=(batch, nt),
        in_specs=[pl.BlockSpec((CONV_WIDTH, SUBLANES, ch), lambda b, i: (0, 0, 0)),
                  row_spec, row_spec, row_spec,
                  pl.BlockSpec((CONV_HALO, ch),
                               lambda b, i: (jnp.maximum((b * nt + i) * halo_per_tile - 1, 0), 0)),
                  pl.BlockSpec((tt, ch), lambda b, i: (b * nt + i, 0))],
        out_specs=pl.BlockSpec((tt, ch), lambda b, i: (b * nt + i, 0)),
        scratch_shapes=[pltpu.VMEM((SUBLANES, CONV_HALO + tt, ch), F32), pltpu.VMEM((tt, ch), F32)],
        compiler_params=_params(("parallel", "arbitrary")),
        name="conv_module",
    )(conv_w, vec(conv_b), vec(ln_g), vec(ln_b), u, u)


def _retention_kernel(lg_ref, g_ref, q_ref, k_ref, v_ref, gate_ref, o_ref, state_ref, *, chunk, seq, group):
    scale = RET_HEAD_DIM ** -0.5
    row = lax.broadcasted_iota(jnp.int32, (chunk, chunk), 0)
    col = lax.broadcasted_iota(jnp.int32, (chunk, chunk), 1)
    rel = (row - col).astype(F32)
    idx = lax.broadcasted_iota(jnp.int32, (chunk, 1), 0).astype(F32)
    heads = [slice(g * HEAD_W, (g + 1) * HEAD_W) for g in range(group)]
    decays = []
    for g in range(group):
        lg = lg_ref[g][:, 0:1]
        decays.append((jnp.where(rel >= 0, jnp.exp(lg * jnp.maximum(rel, 0.0)), 0.0) * scale,
                       jnp.exp(lg * (idx + 1.0)) * scale,
                       jnp.exp(lg * (chunk - 1.0 - idx)),
                       jnp.exp(lg * chunk)))
    state_ref[...] = jnp.zeros_like(state_ref)
    for n in range(seq // chunk):
        sl = slice(n * chunk, (n + 1) * chunk)
        for g in range(group):
            decay, q_decay, k_decay, chunk_decay = decays[g]
            q = q_ref[sl, heads[g]]
            k = k_ref[sl, heads[g]]
            v = v_ref[sl, heads[g]]
            scores = lax.dot_general(q, k, (((1,), (1,)), ((), ())), preferred_element_type=F32) * decay
            y = jnp.dot(scores.astype(BF16), v, preferred_element_type=F32)
            state = state_ref[g]
            y = y + jnp.dot((q.astype(F32) * q_decay).astype(BF16), state.astype(BF16),
                            preferred_element_type=F32)
            kd_t = (k.astype(F32) * k_decay).T.astype(BF16)
            state_ref[g] = state * chunk_decay + jnp.dot(kd_t, v, preferred_element_type=F32)
            mu = jnp.mean(y, axis=-1, keepdims=True)
            var = jnp.mean(jnp.square(y - mu), axis=-1, keepdims=True)
            y = (y - mu) * lax.rsqrt(var + EPS) * g_ref[:, heads[g]]
            o_ref[sl, heads[g]] = (gate_ref[sl, heads[g]] * y).astype(o_ref.dtype)


def _retention(qk, v, gate, gn_g, batch, seq, chunk=256, group=2):
    log_gamma = jnp.log1p(-jnp.exp2(-5.0 - jnp.arange(RET_HEADS, dtype=F32)))
    lg = jnp.broadcast_to(log_gamma[:, None, None], (RET_HEADS, 1, LANES))
    n_groups = RET_HEADS // group
    gw = group * HEAD_W
    head = lambda b, h: (b, h)
    return pl.pallas_call(
        functools.partial(_retention_kernel, chunk=chunk, seq=seq, group=group),
        out_shape=jax.ShapeDtypeStruct((batch * seq, RET_HEADS * HEAD_W), BF16),
        grid=(batch, n_groups),
        in_specs=[pl.BlockSpec((group, 1, LANES), lambda b, h: (h, 0, 0)),
                  pl.BlockSpec((1, gw), lambda b, h: (0, h)),
                  pl.BlockSpec((seq, gw), head),
                  pl.BlockSpec((seq, gw), lambda b, h: (b, n_groups + h)),
                  pl.BlockSpec((seq, gw), head),
                  pl.BlockSpec((seq, gw), head)],
        out_specs=pl.BlockSpec((seq, gw), head),
        scratch_shapes=[pltpu.VMEM((group, RET_HEAD_DIM, RET_HEAD_DIM), F32)],
        compiler_params=_params(("parallel", "parallel")),
        name="retention",
    )(lg, gn_g.reshape(1, -1), qk, qk, v, gate)


def _merge_out_kernel(ya_ref, yb_ref, yr_ref, wbr_ref, g_ref, wo_ref, r_ref, gain_ref,
                      x_ref, xb_ref, inv_ref):
    d = x_ref.shape[1]
    merged = None
    for b, y_ref in enumerate((ya_ref, yb_ref, yr_ref)):
        term = g_ref[:, b * d:(b + 1) * d].astype(F32) * jnp.dot(y_ref[...], wbr_ref[b],
                                                                preferred_element_type=F32)
        merged = term if merged is None else merged + term
    x = r_ref[...] + jnp.dot(merged.astype(BF16), wo_ref[...], preferred_element_type=F32)
    x_ref[...] = x
    xb_ref[...] = (x * gain_ref[...]).astype(BF16)
    inv_ref[...] = _inv_rms_lanes(jnp.sum(x * x, axis=-1, keepdims=True), d)


def _merge_out(ya, yb, yr, w_branch, gates, w_out, res, g_next, tm=256):
    m, kdim = ya.shape
    d = res.shape[1]
    rows = lambda i: (i, 0)
    once = pl.Buffered(1)
    return pl.pallas_call(
        _merge_out_kernel,
        out_shape=(jax.ShapeDtypeStruct((m, d), F32), jax.ShapeDtypeStruct((m, d), BF16),
                   jax.ShapeDtypeStruct((m, LANES), F32)),
        grid=(m // tm,),
        in_specs=[pl.BlockSpec((tm, kdim), rows), pl.BlockSpec((tm, kdim), rows),
                  pl.BlockSpec((tm, kdim), rows),
                  pl.BlockSpec((N_BRANCH, kdim, d), lambda i: (0, 0, 0), pipeline_mode=once),
                  pl.BlockSpec((tm, N_BRANCH * d), rows),
                  pl.BlockSpec((d, d), lambda i: (0, 0), pipeline_mode=once),
                  pl.BlockSpec((tm, d), rows),
                  pl.BlockSpec((1, d), lambda i: (0, 0))],
        out_specs=(pl.BlockSpec((tm, d), rows), pl.BlockSpec((tm, d), rows),
                   pl.BlockSpec((tm, LANES), rows)),
        compiler_params=_params(("parallel",)),
        name="merge_out",
    )(ya, yb, yr, w_branch, gates, w_out, res, g_next.reshape(1, d))


def _xattn_kernel(q_ref, k_ref, v_ref, o_ref, *, scale, hd):
    heads = [slice(h * hd, (h + 1) * hd) for h in range(XATTN_HEADS)]
    logits = [lax.dot_general(q_ref[:, hs], k_ref[:, hs], (((1,), (1,)), ((), ())),
                              preferred_element_type=F32) for hs in heads]
    for hs, s in zip(heads, logits):
        s = s * scale
        p = jnp.exp(s - jnp.max(s, axis=-1, keepdims=True))
        l = jnp.sum(p, axis=-1, keepdims=True)
        o = jnp.dot(p.astype(BF16), v_ref[:, hs], preferred_element_type=F32) / l
        o_ref[:, hs] = o.astype(o_ref.dtype)


def _cross_attention(q, kv, batch, seq, mem_tokens, tq=512):
    d = q.shape[1]
    hd = d // XATTN_HEADS
    nq = seq // tq
    return pl.pallas_call(
        functools.partial(_xattn_kernel, scale=hd ** -0.5, hd=hd),
        out_shape=jax.ShapeDtypeStruct((batch * seq, d), BF16),
        grid=(batch, nq),
        in_specs=[pl.BlockSpec((tq, d), lambda b, i: (b * nq + i, 0)),
                  pl.BlockSpec((mem_tokens, d), lambda b, i: (b, 0)),
                  pl.BlockSpec((mem_tokens, d), lambda b, i: (b, 1))],
        out_specs=pl.BlockSpec((tq, d), lambda b, i: (b * nq + i, 0)),
        compiler_params=_params(("parallel", "arbitrary")),
        name="cross_attention",
    )(q, kv, kv)


def _rope_tables(seq, half, inv_freq):
    ang = jnp.arange(seq, dtype=F32)[:, None] * inv_freq[None, :]
    cos, sin = jnp.cos(ang), jnp.sin(ang)
    reps = LANES // (2 * half)
    cos_l = jnp.tile(jnp.concatenate([cos, cos], axis=1), (1, reps))
    sin_l = jnp.tile(jnp.concatenate([-sin, sin], axis=1), (1, reps))
    return cos_l, sin_l


def kernel(x, mem, norm_mix, w_in, diff_lambda, diff_subln, conv_w, conv_b, conv_ln_g, conv_ln_b,
           ret_gn_g, w_branch, w_out, norm_xattn, norm_mem, xattn_wq, xattn_wkv, xattn_wo,
           norm_ffn, ffn_w13, ffn_w2, norm_final):
    batch, seq, d = x.shape
    mem_tokens = mem.shape[1]
    depth = w_in.shape[0]
    mix_w = d // 2
    ffn_hidden = ffn_w2.shape[1]
    tm = 1024
    tn = 1024
    rows_per_seq = seq // tm

    diff_inv = ROPE_THETA ** (-jnp.arange(0, DIFF_HEAD_DIM, 2, dtype=F32) / DIFF_HEAD_DIM)
    diff_tabs = _rope_tables(seq, DIFF_HEAD_DIM // 2, diff_inv)
    ret_inv = 1.0 / (ROPE_THETA ** jnp.linspace(0.0, 1.0, RET_HEAD_DIM // 2, dtype=F32))
    ret_tabs = _rope_tables(seq, RET_HEAD_DIM // 2, ret_inv)
    tab_spec = pl.BlockSpec((tm, LANES), lambda j, i: (i % rows_per_seq, 0))
    tn_glu = 512

    xf = x.reshape(batch * seq, d)
    memf = mem.reshape(batch * mem_tokens, d)
    seg = {"dq": 0, "dv": 2 * mix_w, "ca": 3 * mix_w, "cb": 4 * mix_w, "rq": 5 * mix_w, "rv": 7 * mix_w,
           "rg": 8 * mix_w, "gates": 9 * mix_w}

    def off(name, width=tn):
        return lambda j, o=seg[name] // width: o + j

    ident = lambda j: j

    xb, inv = _prep_norm(xf, norm_mix[0])

    for i in range(depth):
        lambda_init = 0.8 - 0.6 * math.exp(-0.3 * i)
        last = i == depth - 1

        dqk = _matmul(xb, w_in, i, [off("dq")], 2 * mix_w,
                      functools.partial(_epi_rope, half=DIFF_HEAD_DIM // 2), BF16, inv=inv,
                      aux=diff_tabs, aux_specs=(tab_spec, tab_spec), tm=tm, tn=tn)
        rqk = _matmul(xb, w_in, i, [off("rq")], 2 * mix_w,
                      functools.partial(_epi_rope, half=RET_HEAD_DIM // 2), BF16, inv=inv,
                      aux=ret_tabs, aux_specs=(tab_spec, tab_spec), tm=tm, tn=tn)
        dv = _matmul(xb, w_in, i, [off("dv")], mix_w, _epi_cast, BF16, inv=inv, tm=tm, tn=tn)
        rv = _matmul(xb, w_in, i, [off("rv")], mix_w, _epi_cast, BF16, inv=inv, tm=tm, tn=tn)
        u = _matmul(xb, w_in, i, [off("ca", tn_glu), off("cb", tn_glu)], mix_w, _epi_glu, F32, inv=inv,
                    tm=tm, tn=tn_glu)
        rgate = _matmul(xb, w_in, i, [off("rg")], mix_w, _epi_silu, F32, inv=inv, tm=tm, tn=tn)
        gates = _matmul(xb, w_in, i, [off("gates")], N_BRANCH * d, _epi_sigmoid, BF16, inv=inv,
                        tm=tm, tn=tn)

        ya = _diff_attention(dqk, dv, diff_lambda[i], diff_subln[i], batch, seq, lambda_init)
        yb = _conv_module(u, conv_w[i], conv_b[i], conv_ln_g[i], conv_ln_b[i], batch, seq)
        yr = _retention(rqk, rv, rgate, ret_gn_g[i], batch, seq)

        xf, xb, inv = _merge_out(ya, yb, yr, w_branch[i].astype(BF16), gates, w_out[i].astype(BF16), xf,
                                 norm_xattn[i])

        hm = _rmsnorm(memf, norm_mem[i], BF16, tm=256)
        q = _matmul(xb, xattn_wq, i, [ident], d, _epi_cast, BF16, inv=inv, tm=tm, tn=tn)
        kv = _matmul(hm, xattn_wkv, i, [ident], 2 * d, _epi_cast, BF16,
                     tm=min(tm, batch * mem_tokens), tn=tn)
        o = _cross_attention(q, kv, batch, seq, mem_tokens)
        xf, xb, ssq = _matmul_residual(o, xattn_wo, i, xf, norm_ffn[i], tm=tm, tn=tn)
        inv = _finish_inv(ssq, d)

        act = _matmul(xb, ffn_w13, i, [ident, lambda j, o=ffn_hidden // tn_glu: o + j], ffn_hidden,
                      _epi_swiglu, BF16, inv=inv, tm=tm, tn=tn_glu)
        g_next = norm_final if last else norm_mix[i + 1]
        outs = _matmul_rows_residual(act, ffn_w2[i].astype(BF16), xf, g_next, not last, tm=tm, tn=tn_glu)
        if last:
            xf = outs[0]
        else:
            xf, xb, ssq = outs
            inv = _finish_inv(ssq, d)

    out = _rmsnorm(xf, norm_final, F32)
    return out.reshape(batch, seq, d)
```

```python
import functools
import math

import jax
import jax.numpy as jnp
from jax import lax
from jax.experimental import pallas as pl
from jax.experimental.pallas import tpu as pltpu

F32 = jnp.float32
BF16 = jnp.bfloat16

DIFF_HEADS = 8
DIFF_HEAD_DIM = 64
RET_HEADS = 8
RET_HEAD_DIM = 128
HEAD_W = 128
CONV_WIDTH = 31
N_BRANCH = 3
XATTN_HEADS = 4
ROPE_THETA = 10000.0
EPS = 1e-6
NEG_INF = -1e30

V7X_VMEM_BYTES = 64 * 1024 * 1024
VMEM_LIMIT_BYTES = V7X_VMEM_BYTES - 12 * 1024 * 1024
LANES = 128
SUBLANES = 8


def _params(semantics):
    return pltpu.CompilerParams(dimension_semantics=semantics, vmem_limit_bytes=VMEM_LIMIT_BYTES)


def _rmsnorm_kernel(x_ref, g_ref, o_ref):
    x = x_ref[...]
    y = x * lax.rsqrt(jnp.mean(x * x, axis=-1, keepdims=True) + EPS)
    o_ref[...] = (y * g_ref[...]).astype(o_ref.dtype)


def _rmsnorm(x, g, out_dtype, tm=512):
    m, d = x.shape
    return pl.pallas_call(
        _rmsnorm_kernel,
        out_shape=jax.ShapeDtypeStruct((m, d), out_dtype),
        grid=(m // tm,),
        in_specs=[pl.BlockSpec((tm, d), lambda i: (i, 0)),
                  pl.BlockSpec((1, d), lambda i: (0, 0))],
        out_specs=pl.BlockSpec((tm, d), lambda i: (i, 0)),
        compiler_params=_params(("parallel",)),
        name="rmsnorm",
    )(x, g.reshape(1, d))


def _inv_rms_lanes(sum_sq, d):
    return jnp.broadcast_to(lax.rsqrt(sum_sq * (1.0 / d) + EPS), (sum_sq.shape[0], LANES))


def _prep_norm_kernel(x_ref, g_ref, xb_ref, inv_ref):
    x = x_ref[...]
    xb_ref[...] = (x * g_ref[...]).astype(xb_ref.dtype)
    inv_ref[...] = _inv_rms_lanes(jnp.sum(x * x, axis=-1, keepdims=True), x.shape[1])


def _prep_norm(x, g, tm=512):
    m, d = x.shape
    return pl.pallas_call(
        _prep_norm_kernel,
        out_shape=(jax.ShapeDtypeStruct((m, d), BF16), jax.ShapeDtypeStruct((m, LANES), F32)),
        grid=(m // tm,),
        in_specs=[pl.BlockSpec((tm, d), lambda i: (i, 0)),
                  pl.BlockSpec((1, d), lambda i: (0, 0))],
        out_specs=(pl.BlockSpec((tm, d), lambda i: (i, 0)), pl.BlockSpec((tm, LANES), lambda i: (i, 0))),
        compiler_params=_params(("parallel",)),
        name="prep_norm",
    )(x, g.reshape(1, d))


def _epi_cast(accs, aux):
    return accs[0]


def _sigmoid(x):
    return 0.5 * jnp.tanh(0.5 * x) + 0.5


def _epi_sigmoid(accs, aux):
    return _sigmoid(accs[0])


def _epi_silu(accs, aux):
    return accs[0] * _sigmoid(accs[0])


def _epi_glu(accs, aux):
    return accs[0] * _sigmoid(accs[1])


def _epi_swiglu(accs, aux):
    return accs[0] * _sigmoid(accs[0]) * accs[1]


def _epi_rope(accs, aux, *, half):
    acc = accs[0]
    cos, sin_signed = aux[0][...], aux[1][...]
    tm, tn = acc.shape
    lane = lax.broadcasted_iota(jnp.int32, (tm, LANES), 1)
    first_half = (lane % (2 * half)) < half
    outs = []
    for c in range(tn // LANES):
        x = acc[:, c * LANES:(c + 1) * LANES]
        if 2 * half == LANES:
            partner = pltpu.roll(x, half, 1)
        else:
            partner = jnp.where(first_half, pltpu.roll(x, LANES - half, 1), pltpu.roll(x, half, 1))
        outs.append(x * cos + partner * sin_signed)
    return jnp.concatenate(outs, axis=1)


def _cast_weights_once(w_refs, wb_refs):
    @pl.when(pl.program_id(1) == 0)
    def _():
        for w, wb in zip(w_refs, wb_refs):
            wb[...] = w[0].astype(BF16)


def _mm_kernel(*refs, n_w, epilogue, row_scaled):
    a_ref = refs[0]
    w_refs = refs[1:1 + n_w]
    aux_refs = refs[1 + n_w:-1 - n_w]
    o_ref = refs[-1 - n_w]
    wb_refs = refs[-n_w:]
    _cast_weights_once(w_refs, wb_refs)
    a = a_ref[...]
    accs = [jnp.dot(a, wb[...], preferred_element_type=F32) for wb in wb_refs]
    if row_scaled:
        inv = jnp.tile(aux_refs[0][...], (1, o_ref.shape[1] // LANES))
        accs = [acc * inv for acc in accs]
        aux_refs = aux_refs[1:]
    o_ref[...] = epilogue(accs, aux_refs).astype(o_ref.dtype)


def _matmul(a, w, layer, col_blocks, n_out, epilogue, out_dtype, inv=None, aux=(), aux_specs=(),
            tm=1024, tn=1024):
    m, k = a.shape
    n_w = len(col_blocks)
    in_specs = [pl.BlockSpec((tm, k), lambda j, i: (i, 0))]
    for cb in col_blocks:
        in_specs.append(pl.BlockSpec((1, k, tn), lambda j, i, cb=cb: (layer, 0, cb(j))))
    if inv is not None:
        in_specs.append(pl.BlockSpec((tm, LANES), lambda j, i: (i, 0)))
        aux = (inv,) + tuple(aux)
    in_specs.extend(aux_specs)
    return pl.pallas_call(
        functools.partial(_mm_kernel, n_w=n_w, epilogue=epilogue, row_scaled=inv is not None),
        out_shape=jax.ShapeDtypeStruct((m, n_out), out_dtype),
        grid=(n_out // tn, m // tm),
        in_specs=in_specs,
        out_specs=pl.BlockSpec((tm, tn), lambda j, i: (i, j)),
        scratch_shapes=[pltpu.VMEM((k, tn), BF16) for _ in range(n_w)],
        compiler_params=_params(("arbitrary", "arbitrary")),
        name="matmul_" + getattr(epilogue, "__name__", "rope"),
    )(a, *([w] * n_w), *aux)


def _emit_rows(x, gain_ref, out_refs, final):
    d = x.shape[1]
    sum_sq = jnp.sum(x * x, axis=-1, keepdims=True)
    if final:
        out_refs[0][...] = x * lax.rsqrt(sum_sq * (1.0 / d) + EPS) * gain_ref[...]
    else:
        out_refs[0][...] = x
        out_refs[1][...] = (x * gain_ref[...]).astype(BF16)
        out_refs[2][...] = _inv_rms_lanes(sum_sq, d)


def _rows_out(m, d, tm, final):
    rows = lambda i: (i, 0)
    shapes = [jax.ShapeDtypeStruct((m, d), F32)]
    specs = [pl.BlockSpec((tm, d), rows)]
    if not final:
        shapes += [jax.ShapeDtypeStruct((m, d), BF16), jax.ShapeDtypeStruct((m, LANES), F32)]
        specs += [pl.BlockSpec((tm, d), rows), pl.BlockSpec((tm, LANES), rows)]
    return tuple(shapes), tuple(specs)


def _mm_whole_rows_kernel(a_ref, w_ref, r_ref, gain_ref, *out_refs, final):
    x = r_ref[...] + jnp.dot(a_ref[...], w_ref[...], preferred_element_type=F32)
    _emit_rows(x, gain_ref, out_refs, final)


def _matmul_whole_rows(a, w, res, gain, final, tm):
    m, k = a.shape
    d = w.shape[1]
    rows = lambda i: (i, 0)
    shapes, specs = _rows_out(m, d, tm, final)
    return pl.pallas_call(
        functools.partial(_mm_whole_rows_kernel, final=final),
        out_shape=shapes,
        grid=(m // tm,),
        in_specs=[pl.BlockSpec((tm, k), rows),
                  pl.BlockSpec((k, d), lambda i: (0, 0), pipeline_mode=pl.Buffered(1)),
                  pl.BlockSpec((tm, d), rows),
                  pl.BlockSpec((1, d), lambda i: (0, 0))],
        out_specs=specs,
        compiler_params=_params(("parallel",)),
        name="matmul_whole_rows",
    )(a, w, res, gain.reshape(1, d))


def _diff_attn_kernel(lam_ref, sub_ref, q_ref, k_ref, v_ref, o_ref, vt_ref, acc_ref, *, tq, group,
                      lambda_init):
    qi = pl.program_id(2)
    tk = tq
    nkv = v_ref.shape[0] // tk
    heads = [slice(g * HEAD_W, (g + 1) * HEAD_W) for g in range(group)]

    @pl.when(qi == 0)
    def _():
        for g in range(group):
            for jb in range(nkv):
                vt_ref[g, jb] = v_ref[jb * tk:(jb + 1) * tk, heads[g]].astype(F32).T.astype(BF16)

    scale = DIFF_HEAD_DIM ** -0.5 * math.log2(math.e)
    dim = lax.broadcasted_iota(jnp.int32, (HEAD_W, tq), 0)
    qsts = []
    for g in range(group):
        qt = (q_ref[:, heads[g]].astype(F32) * scale).T
        qsts.append(jnp.concatenate([jnp.where(dim < DIFF_HEAD_DIM, qt, 0.0),
                                     jnp.where(dim >= DIFF_HEAD_DIM, qt, 0.0)], axis=1).astype(BF16))

    def block(j, carry, masked):
        new = []
        logits = [jnp.dot(k_ref[pl.ds(pl.multiple_of(j * tk, tk), tk), heads[g]], qsts[g],
                          preferred_element_type=F32) for g in range(group)]
        for g in range(group):
            m_prev, l_prev = carry[g]
            s = logits[g]
            if masked:
                key = lax.broadcasted_iota(jnp.int32, (tk, 2 * tq), 0)
                qry = lax.broadcasted_iota(jnp.int32, (tk, 2 * tq), 1)
                qry = jnp.where(qry >= tq, qry - tq, qry)
                s = jnp.where(key <= qry, s, NEG_INF)
            m_new = jnp.maximum(m_prev, jnp.max(s, axis=0, keepdims=True))
            alpha = jnp.exp2(m_prev - m_new)
            p = jnp.exp2(s - m_new)
            l_new = alpha * l_prev + jnp.sum(p, axis=0, keepdims=True)
            acc_ref[g] = alpha * acc_ref[g] + jnp.dot(vt_ref[g, j], p.astype(BF16),
                                                      preferred_element_type=F32)
            new.append((m_new, l_new))
        return tuple(new)

    acc_ref[...] = jnp.zeros_like(acc_ref)
    init = tuple((jnp.full((1, 2 * tq), NEG_INF, F32), jnp.zeros((1, 2 * tq), F32)) for _ in range(group))
    carry = lax.fori_loop(0, qi, functools.partial(block, masked=False), init)
    carry = block(qi, carry, masked=True)

    lam_p = lam_ref[...]
    lam = (jnp.exp(jnp.sum(lam_p[0:1] * lam_p[1:2], axis=-1, keepdims=True))
           - jnp.exp(jnp.sum(lam_p[2:3] * lam_p[3:4], axis=-1, keepdims=True)) + lambda_init)
    for g in range(group):
        o = acc_ref[g] * (1.0 / carry[g][1])
        y = o[:, :tq] - lam * o[:, tq:]
        y = y * lax.rsqrt(jnp.mean(y * y, axis=0, keepdims=True) + EPS) * sub_ref[...]
        o_ref[:, heads[g]] = (y * (1.0 - lambda_init)).T.astype(o_ref.dtype)


def _diff_attention(qk, v, diff_lambda, diff_subln, batch, seq, lambda_init, tq=256, group=8):
    nq = seq // tq
    n_groups = DIFF_HEADS // group
    gw = group * HEAD_W
    return pl.pallas_call(
        functools.partial(_diff_attn_kernel, tq=tq, group=group, lambda_init=lambda_init),
        out_shape=jax.ShapeDtypeStruct((batch * seq, DIFF_HEADS * HEAD_W), BF16),
        grid=(batch, n_groups, nq),
        in_specs=[pl.BlockSpec((4, DIFF_HEAD_DIM), lambda b, h, i: (0, 0)),
                  pl.BlockSpec((HEAD_W, 1), lambda b, h, i: (0, 0)),
                  pl.BlockSpec((tq, gw), lambda b, h, i: (b * nq + i, h)),
                  pl.BlockSpec((seq, gw), lambda b, h, i: (b, n_groups + h)),
                  pl.BlockSpec((seq, gw), lambda b, h, i: (b, h))],
        out_specs=pl.BlockSpec((tq, gw), lambda b, h, i: (b * nq + i, h)),
        scratch_shapes=[pltpu.VMEM((group, nq, HEAD_W, tq), BF16),
                        pltpu.VMEM((group, HEAD_W, 2 * tq), F32)],
        compiler_params=_params(("parallel", "parallel", "arbitrary")),
        name="diff_attention",
    )(diff_lambda, diff_subln.reshape(HEAD_W, 1), qk, qk, v)


CONV_HALO = 32
LN_ROWS = 64


def _conv_kernel(w_ref, b_ref, g_ref, beta_ref, prev_ref, cur_ref, o_ref, sh_ref, y_ref, *, tt, rows,
                 lanes):
    i = pl.program_id(1)
    ch = sh_ref.shape[2]
    n_ext = CONV_HALO + tt

    @pl.when(i == 0)
    def _():
        sh_ref[0, 0:CONV_HALO, :] = jnp.zeros((CONV_HALO, ch), F32)

    @pl.when(i > 0)
    def _():
        sh_ref[0, 0:CONV_HALO, :] = prev_ref[...]

    sh_ref[0, CONV_HALO:n_ext, :] = cur_ref[...]
    for c0 in range(0, ch, LANES):
        ext = sh_ref[0, :, c0:c0 + LANES]
        for s in range(1, SUBLANES):
            sh_ref[s, 0:n_ext - SUBLANES, c0:c0 + LANES] = pltpu.roll(ext, n_ext - s, 0)[0:n_ext - SUBLANES]

    first = CONV_HALO - (CONV_WIDTH - 1)
    groups = rows // SUBLANES

    def row_chunk(c, carry):
        r0 = pl.multiple_of(c * rows, rows)
        for c0 in range(0, ch, lanes):
            acc = jnp.zeros((groups, SUBLANES, lanes), F32)
            for s in range(SUBLANES):
                offs = [o for o in range(first, first + CONV_WIDTH) if o % SUBLANES == s]
                a_lo, a_hi = offs[0] // SUBLANES, offs[-1] // SUBLANES
                span = sh_ref[s, pl.ds(r0 + SUBLANES * a_lo, rows + SUBLANES * (a_hi - a_lo)), c0:c0 + lanes]
                span = span.reshape(groups + a_hi - a_lo, SUBLANES, lanes)
                for o in offs:
                    a = o // SUBLANES - a_lo
                    acc = acc + w_ref[o - first, :, c0:c0 + lanes] * span[a:a + groups]
            y_ref[pl.ds(r0, rows), c0:c0 + lanes] = acc.reshape(rows, lanes)
        return carry

    lax.fori_loop(0, tt // rows, row_chunk, 0)

    for r0 in range(0, tt, LN_ROWS):
        y = y_ref[r0:r0 + LN_ROWS, :] + b_ref[...]
        mu = jnp.mean(y, axis=-1, keepdims=True)
        var = jnp.mean(jnp.square(y - mu), axis=-1, keepdims=True)
        y = (y - mu) * lax.rsqrt(var + EPS) * g_ref[...] + beta_ref[...]
        o_ref[r0:r0 + LN_ROWS, :] = (y * _sigmoid(y)).astype(o_ref.dtype)


def _conv_module(u, conv_w, conv_b, ln_g, ln_b, batch, seq, tt=256, rows=32, lanes=256):
    ch = u.shape[1]
    conv_w = jnp.broadcast_to(conv_w[:, None, :], (CONV_WIDTH, SUBLANES, ch))
    nt = seq // tt
    halo_per_tile = tt // CONV_HALO
    vec = lambda a: a.reshape(1, ch)
    row_spec = pl.BlockSpec((1, ch), lambda b, i: (0, 0))
    return pl.pallas_call(
        functools.partial(_conv_kernel, tt=tt, rows=rows, lanes=lanes),
        out_shape=jax.ShapeDtypeStruct((batch * seq, ch), BF16),
        grid=(batch, nt),
        in_specs=[pl.BlockSpec((CONV_WIDTH, SUBLANES, ch), lambda b, i: (0, 0, 0)),
                  row_spec, row_spec, row_spec,
                  pl.BlockSpec((CONV_HALO, ch),
                               lambda b, i: (jnp.maximum((b * nt + i) * halo_per_tile - 1, 0), 0)),
                  pl.BlockSpec((tt, ch), lambda b, i: (b * nt + i, 0))],
        out_specs=pl.BlockSpec((tt, ch), lambda b, i: (b * nt + i, 0)),
        scratch_shapes=[pltpu.VMEM((SUBLANES, CONV_HALO + tt, ch), F32), pltpu.VMEM((tt, ch), F32)],
        compiler_params=_params(("parallel", "arbitrary")),
        name="conv_module",
    )(conv_w, vec(conv_b), vec(ln_g), vec(ln_b), u, u)


def _retention_kernel(lg_ref, g_ref, q_ref, k_ref, v_ref, gate_ref, o_ref, state_ref, *, chunk, seq, group):
    scale = RET_HEAD_DIM ** -0.5
    row = lax.broadcasted_iota(jnp.int32, (chunk, chunk), 0)
    col = lax.broadcasted_iota(jnp.int32, (chunk, chunk), 1)
    rel = (row - col).astype(F32)
    idx = lax.broadcasted_iota(jnp.int32, (chunk, 1), 0).astype(F32)
    heads = [slice(g * HEAD_W, (g + 1) * HEAD_W) for g in range(group)]
    decays = []
    for g in range(group):
        lg = lg_ref[g][:, 0:1]
        decays.append((jnp.where(rel >= 0, jnp.exp(lg * jnp.maximum(rel, 0.0)), 0.0) * scale,
                       jnp.exp(lg * (idx + 1.0)) * scale,
                       jnp.exp(lg * (chunk - 1.0 - idx)),
                       jnp.exp(lg * chunk)))
    state_ref[...] = jnp.zeros_like(state_ref)
    for n in range(seq // chunk):
        sl = slice(n * chunk, (n + 1) * chunk)
        for g in range(group):
            decay, q_decay, k_decay, chunk_decay = decays[g]
            q = q_ref[sl, heads[g]]
            k = k_ref[sl, heads[g]]
            v = v_ref[sl, heads[g]]
            scores = lax.dot_general(q, k, (((1,), (1,)), ((), ())), preferred_element_type=F32) * decay
            y = jnp.dot(scores.astype(BF16), v, preferred_element_type=F32)
            state = state_ref[g]
            y = y + jnp.dot((q.astype(F32) * q_decay).astype(BF16), state.astype(BF16),
                            preferred_element_type=F32)
            kd_t = (k.astype(F32) * k_decay).T.astype(BF16)
            state_ref[g] = state * chunk_decay + jnp.dot(kd_t, v, preferred_element_type=F32)
            mu = jnp.mean(y, axis=-1, keepdims=True)
            var = jnp.mean(jnp.square(y - mu), axis=-1, keepdims=True)
            y = (y - mu) * lax.rsqrt(var + EPS) * g_ref[:, heads[g]]
            o_ref[sl, heads[g]] = (gate_ref[sl, heads[g]] * y).astype(o_ref.dtype)


def _retention(qk, v, gate, gn_g, batch, seq, chunk=256, group=2):
    log_gamma = jnp.log1p(-jnp.exp2(-5.0 - jnp.arange(RET_HEADS, dtype=F32)))
    lg = jnp.broadcast_to(log_gamma[:, None, None], (RET_HEADS, 1, LANES))
    n_groups = RET_HEADS // group
    gw = group * HEAD_W
    head = lambda b, h: (b, h)
    return pl.pallas_call(
        functools.partial(_retention_kernel, chunk=chunk, seq=seq, group=group),
        out_shape=jax.ShapeDtypeStruct((batch * seq, RET_HEADS * HEAD_W), BF16),
        grid=(batch, n_groups),
        in_specs=[pl.BlockSpec((group, 1, LANES), lambda b, h: (h, 0, 0)),
                  pl.BlockSpec((1, gw), lambda b, h: (0, h)),
                  pl.BlockSpec((seq, gw), head),
                  pl.BlockSpec((seq, gw), lambda b, h: (b, n_groups + h)),
                  pl.BlockSpec((seq, gw), head),
                  pl.BlockSpec((seq, gw), head)],
        out_specs=pl.BlockSpec((seq, gw), head),
        scratch_shapes=[pltpu.VMEM((group, RET_HEAD_DIM, RET_HEAD_DIM), F32)],
        compiler_params=_params(("parallel", "parallel")),
        name="retention",
    )(lg, gn_g.reshape(1, -1), qk, qk, v, gate)


def _merge_out_kernel(ya_ref, yb_ref, yr_ref, wbr_ref, g_ref, wo_ref, r_ref, gain_ref,
                      x_ref, xb_ref, inv_ref):
    d = x_ref.shape[1]
    merged = None
    for b, y_ref in enumerate((ya_ref, yb_ref, yr_ref)):
        term = g_ref[:, b * d:(b + 1) * d].astype(F32) * jnp.dot(y_ref[...], wbr_ref[b],
                                                                preferred_element_type=F32)
        merged = term if merged is None else merged + term
    x = r_ref[...] + jnp.dot(merged.astype(BF16), wo_ref[...], preferred_element_type=F32)
    _emit_rows(x, gain_ref, (x_ref, xb_ref, inv_ref), final=False)


def _merge_out(ya, yb, yr, w_branch, gates, w_out, res, g_next, tm=256):
    m, kdim = ya.shape
    d = res.shape[1]
    rows = lambda i: (i, 0)
    once = pl.Buffered(1)
    return pl.pallas_call(
        _merge_out_kernel,
        out_shape=(jax.ShapeDtypeStruct((m, d), F32), jax.ShapeDtypeStruct((m, d), BF16),
                   jax.ShapeDtypeStruct((m, LANES), F32)),
        grid=(m // tm,),
        in_specs=[pl.BlockSpec((tm, kdim), rows), pl.BlockSpec((tm, kdim), rows),
                  pl.BlockSpec((tm, kdim), rows),
                  pl.BlockSpec((N_BRANCH, kdim, d), lambda i: (0, 0, 0), pipeline_mode=once),
                  pl.BlockSpec((tm, N_BRANCH * d), rows),
                  pl.BlockSpec((d, d), lambda i: (0, 0), pipeline_mode=once),
                  pl.BlockSpec((tm, d), rows),
                  pl.BlockSpec((1, d), lambda i: (0, 0))],
        out_specs=(pl.BlockSpec((tm, d), rows), pl.BlockSpec((tm, d), rows),
                   pl.BlockSpec((tm, LANES), rows)),
        compiler_params=_params(("parallel",)),
        name="merge_out",
    )(ya, yb, yr, w_branch, gates, w_out, res, g_next.reshape(1, d))


def _xattn_kernel(q_ref, k_ref, v_ref, o_ref, *, scale, hd):
    heads = [slice(h * hd, (h + 1) * hd) for h in range(XATTN_HEADS)]
    logits = [lax.dot_general(q_ref[:, hs], k_ref[:, hs], (((1,), (1,)), ((), ())),
                              preferred_element_type=F32) for hs in heads]
    for hs, s in zip(heads, logits):
        s = s * scale
        p = jnp.exp(s - jnp.max(s, axis=-1, keepdims=True))
        l = jnp.sum(p, axis=-1, keepdims=True)
        o = jnp.dot(p.astype(BF16), v_ref[:, hs], preferred_element_type=F32) / l
        o_ref[:, hs] = o.astype(o_ref.dtype)


def _cross_attention(q, kv, batch, seq, mem_tokens, tq=512):
    d = q.shape[1]
    hd = d // XATTN_HEADS
    nq = seq // tq
    return pl.pallas_call(
        functools.partial(_xattn_kernel, scale=hd ** -0.5, hd=hd),
        out_shape=jax.ShapeDtypeStruct((batch * seq, d), BF16),
        grid=(batch, nq),
        in_specs=[pl.BlockSpec((tq, d), lambda b, i: (b * nq + i, 0)),
                  pl.BlockSpec((mem_tokens, d), lambda b, i: (b, 0)),
                  pl.BlockSpec((mem_tokens, d), lambda b, i: (b, 1))],
        out_specs=pl.BlockSpec((tq, d), lambda b, i: (b * nq + i, 0)),
        compiler_params=_params(("parallel", "arbitrary")),
        name="cross_attention",
    )(q, kv, kv)


def _rope_tables(seq, half, inv_freq):
    ang = jnp.arange(seq, dtype=F32)[:, None] * inv_freq[None, :]
    cos, sin = jnp.cos(ang), jnp.sin(ang)
    reps = LANES // (2 * half)
    cos_l = jnp.tile(jnp.concatenate([cos, cos], axis=1), (1, reps))
    sin_l = jnp.tile(jnp.concatenate([-sin, sin], axis=1), (1, reps))
    return cos_l, sin_l


def kernel(x, mem, norm_mix, w_in, diff_lambda, diff_subln, conv_w, conv_b, conv_ln_g, conv_ln_b,
           ret_gn_g, w_branch, w_out, norm_xattn, norm_mem, xattn_wq, xattn_wkv, xattn_wo,
           norm_ffn, ffn_w13, ffn_w2, norm_final):
    batch, seq, d = x.shape
    mem_tokens = mem.shape[1]
    depth = w_in.shape[0]
    mix_w = d // 2
    ffn_hidden = ffn_w2.shape[1]
    tm = 1024
    tn = 1024
    rows_per_seq = seq // tm

    diff_inv = ROPE_THETA ** (-jnp.arange(0, DIFF_HEAD_DIM, 2, dtype=F32) / DIFF_HEAD_DIM)
    diff_tabs = _rope_tables(seq, DIFF_HEAD_DIM // 2, diff_inv)
    ret_inv = 1.0 / (ROPE_THETA ** jnp.linspace(0.0, 1.0, RET_HEAD_DIM // 2, dtype=F32))
    ret_tabs = _rope_tables(seq, RET_HEAD_DIM // 2, ret_inv)
    tab_spec = pl.BlockSpec((tm, LANES), lambda j, i: (i % rows_per_seq, 0))
    tn_glu = 512

    xf = x.reshape(batch * seq, d)
    memf = mem.reshape(batch * mem_tokens, d)
    seg = {"dq": 0, "dv": 2 * mix_w, "ca": 3 * mix_w, "cb": 4 * mix_w, "rq": 5 * mix_w, "rv": 7 * mix_w,
           "rg": 8 * mix_w, "gates": 9 * mix_w}

    def off(name, width=tn):
        return lambda j, o=seg[name] // width: o + j

    ident = lambda j: j

    xb, inv = _prep_norm(xf, norm_mix[0])

    for i in range(depth):
        lambda_init = 0.8 - 0.6 * math.exp(-0.3 * i)
        last = i == depth - 1

        dqk = _matmul(xb, w_in, i, [off("dq")], 2 * mix_w,
                      functools.partial(_epi_rope, half=DIFF_HEAD_DIM // 2), BF16, inv=inv,
                      aux=diff_tabs, aux_specs=(tab_spec, tab_spec), tm=tm, tn=tn)
        rqk = _matmul(xb, w_in, i, [off("rq")], 2 * mix_w,
                      functools.partial(_epi_rope, half=RET_HEAD_DIM // 2), BF16, inv=inv,
                      aux=ret_tabs, aux_specs=(tab_spec, tab_spec), tm=tm, tn=tn)
        dv = _matmul(xb, w_in, i, [off("dv")], mix_w, _epi_cast, BF16, inv=inv, tm=tm, tn=tn)
        rv = _matmul(xb, w_in, i, [off("rv")], mix_w, _epi_cast, BF16, inv=inv, tm=tm, tn=tn)
        u = _matmul(xb, w_in, i, [off("ca", tn_glu), off("cb", tn_glu)], mix_w, _epi_glu, F32, inv=inv,
                    tm=tm, tn=tn_glu)
        rgate = _matmul(xb, w_in, i, [off("rg")], mix_w, _epi_silu, F32, inv=inv, tm=tm, tn=tn)
        gates = _matmul(xb, w_in, i, [off("gates")], N_BRANCH * d, _epi_sigmoid, BF16, inv=inv,
                        tm=tm, tn=tn)

        ya = _diff_attention(dqk, dv, diff_lambda[i], diff_subln[i], batch, seq, lambda_init)
        yb = _conv_module(u, conv_w[i], conv_b[i], conv_ln_g[i], conv_ln_b[i], batch, seq)
        yr = _retention(rqk, rv, rgate, ret_gn_g[i], batch, seq)

        xf, xb, inv = _merge_out(ya, yb, yr, w_branch[i].astype(BF16), gates, w_out[i].astype(BF16), xf,
                                 norm_xattn[i])

        hm = _rmsnorm(memf, norm_mem[i], BF16, tm=256)
        q = _matmul(xb, xattn_wq, i, [ident], d, _epi_cast, BF16, inv=inv, tm=tm, tn=tn)
        kv = _matmul(hm, xattn_wkv, i, [ident], 2 * d, _epi_cast, BF16,
                     tm=min(tm, batch * mem_tokens), tn=tn)
        o = _cross_attention(q, kv, batch, seq, mem_tokens)
        xf, xb, inv = _matmul_whole_rows(o, xattn_wo[i].astype(BF16), xf, norm_ffn[i], False, tm=512)

        act = _matmul(xb, ffn_w13, i, [ident, lambda j, o=ffn_hidden // tn_glu: o + j], ffn_hidden,
                      _epi_swiglu, BF16, inv=inv, tm=tm, tn=tn_glu)
        if last:
            (out,) = _matmul_whole_rows(act, ffn_w2[i].astype(BF16), xf, norm_final, True, tm=256)
        else:
            xf, xb, inv = _matmul_whole_rows(act, ffn_w2[i].astype(BF16), xf, norm_mix[i + 1], False,
                                             tm=256)

    return out.reshape(batch, seq, d)
```

```python
import functools
import math

import jax
import jax.numpy as jnp
from jax import lax
from jax.experimental import pallas as pl
from jax.experimental.pallas import tpu as pltpu

F32 = jnp.float32
BF16 = jnp.bfloat16

DIFF_HEADS = 8
DIFF_HEAD_DIM = 64
RET_HEADS = 8
RET_HEAD_DIM = 128
HEAD_W = 128
CONV_WIDTH = 31
N_BRANCH = 3
XATTN_HEADS = 4
ROPE_THETA = 10000.0
EPS = 1e-6
NEG_INF = -1e30

V7X_VMEM_BYTES = 64 * 1024 * 1024
VMEM_LIMIT_BYTES = V7X_VMEM_BYTES - 12 * 1024 * 1024
LANES = 128
SUBLANES = 8

TM = 1024
TN = 1024
TN_GATED = 512
TM_ROWS_D = 512
TM_ROWS_WIDE = 256


def _params(semantics):
    return pltpu.CompilerParams(dimension_semantics=semantics, vmem_limit_bytes=VMEM_LIMIT_BYTES)


def _rmsnorm_kernel(x_ref, g_ref, o_ref):
    x = x_ref[...]
    y = x * lax.rsqrt(jnp.mean(x * x, axis=-1, keepdims=True) + EPS)
    o_ref[...] = (y * g_ref[...]).astype(o_ref.dtype)


def _rmsnorm(x, g, out_dtype, tm=512):
    m, d = x.shape
    return pl.pallas_call(
        _rmsnorm_kernel,
        out_shape=jax.ShapeDtypeStruct((m, d), out_dtype),
        grid=(m // tm,),
        in_specs=[pl.BlockSpec((tm, d), lambda i: (i, 0)),
                  pl.BlockSpec((1, d), lambda i: (0, 0))],
        out_specs=pl.BlockSpec((tm, d), lambda i: (i, 0)),
        compiler_params=_params(("parallel",)),
        name="rmsnorm",
    )(x, g.reshape(1, d))


def _inv_rms_lanes(sum_sq, d):
    return jnp.broadcast_to(lax.rsqrt(sum_sq * (1.0 / d) + EPS), (sum_sq.shape[0], LANES))


def _prep_norm_kernel(x_ref, g_ref, xb_ref, inv_ref):
    x = x_ref[...]
    xb_ref[...] = (x * g_ref[...]).astype(xb_ref.dtype)
    inv_ref[...] = _inv_rms_lanes(jnp.sum(x * x, axis=-1, keepdims=True), x.shape[1])


def _prep_norm(x, g, tm=512):
    m, d = x.shape
    return pl.pallas_call(
        _prep_norm_kernel,
        out_shape=(jax.ShapeDtypeStruct((m, d), BF16), jax.ShapeDtypeStruct((m, LANES), F32)),
        grid=(m // tm,),
        in_specs=[pl.BlockSpec((tm, d), lambda i: (i, 0)),
                  pl.BlockSpec((1, d), lambda i: (0, 0))],
        out_specs=(pl.BlockSpec((tm, d), lambda i: (i, 0)), pl.BlockSpec((tm, LANES), lambda i: (i, 0))),
        compiler_params=_params(("parallel",)),
        name="prep_norm",
    )(x, g.reshape(1, d))


def _cast_kernel(w_ref, o_ref):
    o_ref[...] = w_ref[0].astype(o_ref.dtype)


def _cast_layer(w, layer, rows=512):
    _, r, n = w.shape
    return pl.pallas_call(
        _cast_kernel,
        out_shape=jax.ShapeDtypeStruct((r, n), BF16),
        grid=(r // rows,),
        in_specs=[pl.BlockSpec((1, rows, n), lambda i: (layer, i, 0))],
        out_specs=pl.BlockSpec((rows, n), lambda i: (i, 0)),
        compiler_params=_params(("parallel",)),
        name="cast_weights",
    )(w)


def _epi_cast(accs, aux):
    return accs[0]


def _sigmoid(x):
    return 0.5 * jnp.tanh(0.5 * x) + 0.5


def _epi_sigmoid(accs, aux):
    return _sigmoid(accs[0])


def _epi_silu(accs, aux):
    return accs[0] * _sigmoid(accs[0])


def _epi_glu(accs, aux):
    return accs[0] * _sigmoid(accs[1])


def _epi_swiglu(accs, aux):
    return accs[0] * _sigmoid(accs[0]) * accs[1]


def _epi_rope(accs, aux, *, half):
    acc = accs[0]
    cos, sin_signed = aux[0][...], aux[1][...]
    tm, tn = acc.shape
    lane = lax.broadcasted_iota(jnp.int32, (tm, LANES), 1)
    first_half = (lane % (2 * half)) < half
    outs = []
    for c in range(tn // LANES):
        x = acc[:, c * LANES:(c + 1) * LANES]
        if 2 * half == LANES:
            partner = pltpu.roll(x, half, 1)
        else:
            partner = jnp.where(first_half, pltpu.roll(x, LANES - half, 1), pltpu.roll(x, half, 1))
        outs.append(x * cos + partner * sin_signed)
    return jnp.concatenate(outs, axis=1)


def _cast_weights_once(w_refs, wb_refs):
    @pl.when(pl.program_id(1) == 0)
    def _():
        for w, wb in zip(w_refs, wb_refs):
            wb[...] = w[0].astype(BF16)


def _mm_kernel(*refs, n_w, epilogue, row_scaled):
    a_ref = refs[0]
    w_refs = refs[1:1 + n_w]
    aux_refs = refs[1 + n_w:-1 - n_w]
    o_ref = refs[-1 - n_w]
    wb_refs = refs[-n_w:]
    _cast_weights_once(w_refs, wb_refs)
    a = a_ref[...]
    accs = [jnp.dot(a, wb[...], preferred_element_type=F32) for wb in wb_refs]
    if row_scaled:
        inv = jnp.tile(aux_refs[0][...], (1, o_ref.shape[1] // LANES))
        accs = [acc * inv for acc in accs]
        aux_refs = aux_refs[1:]
    o_ref[...] = epilogue(accs, aux_refs).astype(o_ref.dtype)


def _matmul(a, w, layer, col_blocks, n_out, epilogue, out_dtype, inv=None, aux=(), aux_specs=(),
            tm=1024, tn=1024):
    m, k = a.shape
    n_w = len(col_blocks)
    in_specs = [pl.BlockSpec((tm, k), lambda j, i: (i, 0))]
    for cb in col_blocks:
        in_specs.append(pl.BlockSpec((1, k, tn), lambda j, i, cb=cb: (layer, 0, cb(j))))
    if inv is not None:
        in_specs.append(pl.BlockSpec((tm, LANES), lambda j, i: (i, 0)))
        aux = (inv,) + tuple(aux)
    in_specs.extend(aux_specs)
    return pl.pallas_call(
        functools.partial(_mm_kernel, n_w=n_w, epilogue=epilogue, row_scaled=inv is not None),
        out_shape=jax.ShapeDtypeStruct((m, n_out), out_dtype),
        grid=(n_out // tn, m // tm),
        in_specs=in_specs,
        out_specs=pl.BlockSpec((tm, tn), lambda j, i: (i, j)),
        scratch_shapes=[pltpu.VMEM((k, tn), BF16) for _ in range(n_w)],
        compiler_params=_params(("arbitrary", "arbitrary")),
        name="matmul_" + getattr(epilogue, "__name__", "rope"),
    )(a, *([w] * n_w), *aux)


def _emit_rows(x, gain_ref, out_refs, final):
    d = x.shape[1]
    sum_sq = jnp.sum(x * x, axis=-1, keepdims=True)
    if final:
        out_refs[0][...] = x * lax.rsqrt(sum_sq * (1.0 / d) + EPS) * gain_ref[...]
    else:
        out_refs[0][...] = x
        out_refs[1][...] = (x * gain_ref[...]).astype(BF16)
        out_refs[2][...] = _inv_rms_lanes(sum_sq, d)


def _rows_out(m, d, tm, final):
    rows = lambda i: (i, 0)
    shapes = [jax.ShapeDtypeStruct((m, d), F32)]
    specs = [pl.BlockSpec((tm, d), rows)]
    if not final:
        shapes += [jax.ShapeDtypeStruct((m, d), BF16), jax.ShapeDtypeStruct((m, LANES), F32)]
        specs += [pl.BlockSpec((tm, d), rows), pl.BlockSpec((tm, LANES), rows)]
    return tuple(shapes), tuple(specs)


def _mm_whole_rows_kernel(a_ref, w_ref, r_ref, gain_ref, *out_refs, final):
    x = r_ref[...] + jnp.dot(a_ref[...], w_ref[...], preferred_element_type=F32)
    _emit_rows(x, gain_ref, out_refs, final)


def _matmul_whole_rows(a, w, res, gain, final, tm):
    m, k = a.shape
    d = w.shape[1]
    rows = lambda i: (i, 0)
    shapes, specs = _rows_out(m, d, tm, final)
    return pl.pallas_call(
        functools.partial(_mm_whole_rows_kernel, final=final),
        out_shape=shapes,
        grid=(m // tm,),
        in_specs=[pl.BlockSpec((tm, k), rows),
                  pl.BlockSpec((k, d), lambda i: (0, 0), pipeline_mode=pl.Buffered(1)),
                  pl.BlockSpec((tm, d), rows),
                  pl.BlockSpec((1, d), lambda i: (0, 0))],
        out_specs=specs,
        compiler_params=_params(("parallel",)),
        name="matmul_whole_rows",
    )(a, w, res, gain.reshape(1, d))


def _diff_attn_kernel(lam_ref, sub_ref, q_ref, k_ref, v_ref, o_ref, vt_ref, acc_ref, *, tq, group,
                      lambda_init):
    qi = pl.program_id(2)
    tk = tq
    nkv = v_ref.shape[0] // tk
    heads = [slice(g * HEAD_W, (g + 1) * HEAD_W) for g in range(group)]

    @pl.when(qi == 0)
    def _():
        for g in range(group):
            for jb in range(nkv):
                vt_ref[g, jb] = v_ref[jb * tk:(jb + 1) * tk, heads[g]].astype(F32).T.astype(BF16)

    scale = DIFF_HEAD_DIM ** -0.5 * math.log2(math.e)
    dim = lax.broadcasted_iota(jnp.int32, (HEAD_W, tq), 0)
    qsts = []
    for g in range(group):
        qt = (q_ref[:, heads[g]].astype(F32) * scale).T
        qsts.append(jnp.concatenate([jnp.where(dim < DIFF_HEAD_DIM, qt, 0.0),
                                     jnp.where(dim >= DIFF_HEAD_DIM, qt, 0.0)], axis=1).astype(BF16))

    def block(j, carry, masked):
        new = []
        logits = [jnp.dot(k_ref[pl.ds(pl.multiple_of(j * tk, tk), tk), heads[g]], qsts[g],
                          preferred_element_type=F32) for g in range(group)]
        for g in range(group):
            m_prev, l_prev = carry[g]
            s = logits[g]
            if masked:
                key = lax.broadcasted_iota(jnp.int32, (tk, 2 * tq), 0)
                qry = lax.broadcasted_iota(jnp.int32, (tk, 2 * tq), 1)
                qry = jnp.where(qry >= tq, qry - tq, qry)
                s = jnp.where(key <= qry, s, NEG_INF)
            m_new = jnp.maximum(m_prev, jnp.max(s, axis=0, keepdims=True))
            alpha = jnp.exp2(m_prev - m_new)
            p = jnp.exp2(s - m_new)
            l_new = alpha * l_prev + jnp.sum(p, axis=0, keepdims=True)
            acc_ref[g] = alpha * acc_ref[g] + jnp.dot(vt_ref[g, j], p.astype(BF16),
                                                      preferred_element_type=F32)
            new.append((m_new, l_new))
        return tuple(new)

    acc_ref[...] = jnp.zeros_like(acc_ref)
    init = tuple((jnp.full((1, 2 * tq), NEG_INF, F32), jnp.zeros((1, 2 * tq), F32)) for _ in range(group))
    carry = lax.fori_loop(0, qi, functools.partial(block, masked=False), init)
    carry = block(qi, carry, masked=True)

    lam_p = lam_ref[...]
    lam = (jnp.exp(jnp.sum(lam_p[0:1] * lam_p[1:2], axis=-1, keepdims=True))
           - jnp.exp(jnp.sum(lam_p[2:3] * lam_p[3:4], axis=-1, keepdims=True)) + lambda_init)
    for g in range(group):
        o = acc_ref[g] * (1.0 / carry[g][1])
        y = o[:, :tq] - lam * o[:, tq:]
        y = y * lax.rsqrt(jnp.mean(y * y, axis=0, keepdims=True) + EPS) * sub_ref[...]
        o_ref[:, heads[g]] = (y * (1.0 - lambda_init)).T.astype(o_ref.dtype)


def _diff_attention(qk, v, diff_lambda, diff_subln, batch, seq, lambda_init, tq=256, group=8):
    nq = seq // tq
    n_groups = DIFF_HEADS // group
    gw = group * HEAD_W
    return pl.pallas_call(
        functools.partial(_diff_attn_kernel, tq=tq, group=group, lambda_init=lambda_init),
        out_shape=jax.ShapeDtypeStruct((batch * seq, DIFF_HEADS * HEAD_W), BF16),
        grid=(batch, n_groups, nq),
        in_specs=[pl.BlockSpec((4, DIFF_HEAD_DIM), lambda b, h, i: (0, 0)),
                  pl.BlockSpec((HEAD_W, 1), lambda b, h, i: (0, 0)),
                  pl.BlockSpec((tq, gw), lambda b, h, i: (b * nq + i, h)),
                  pl.BlockSpec((seq, gw), lambda b, h, i: (b, n_groups + h)),
                  pl.BlockSpec((seq, gw), lambda b, h, i: (b, h))],
        out_specs=pl.BlockSpec((tq, gw), lambda b, h, i: (b * nq + i, h)),
        scratch_shapes=[pltpu.VMEM((group, nq, HEAD_W, tq), BF16),
                        pltpu.VMEM((group, HEAD_W, 2 * tq), F32)],
        compiler_params=_params(("parallel", "parallel", "arbitrary")),
        name="diff_attention",
    )(diff_lambda, diff_subln.reshape(HEAD_W, 1), qk, qk, v)


CONV_HALO = 32
LN_ROWS = 64


def _conv_kernel(w_ref, b_ref, g_ref, beta_ref, prev_ref, cur_ref, o_ref, sh_ref, y_ref, *, tt, rows,
                 lanes):
    i = pl.program_id(1)
    ch = sh_ref.shape[2]
    n_ext = CONV_HALO + tt

    @pl.when(i == 0)
    def _():
        sh_ref[0, 0:CONV_HALO, :] = jnp.zeros((CONV_HALO, ch), F32)

    @pl.when(i > 0)
    def _():
        sh_ref[0, 0:CONV_HALO, :] = prev_ref[...]

    sh_ref[0, CONV_HALO:n_ext, :] = cur_ref[...]
    for c0 in range(0, ch, LANES):
        ext = sh_ref[0, :, c0:c0 + LANES]
        for s in range(1, SUBLANES):
            sh_ref[s, 0:n_ext - SUBLANES, c0:c0 + LANES] = pltpu.roll(ext, n_ext - s, 0)[0:n_ext - SUBLANES]

    first = CONV_HALO - (CONV_WIDTH - 1)
    groups = rows // SUBLANES

    def row_chunk(c, carry):
        r0 = pl.multiple_of(c * rows, rows)
        for c0 in range(0, ch, lanes):
            acc = jnp.zeros((groups, SUBLANES, lanes), F32)
            for s in range(SUBLANES):
                offs = [o for o in range(first, first + CONV_WIDTH) if o % SUBLANES == s]
                a_lo, a_hi = offs[0] // SUBLANES, offs[-1] // SUBLANES
                span = sh_ref[s, pl.ds(r0 + SUBLANES * a_lo, rows + SUBLANES * (a_hi - a_lo)), c0:c0 + lanes]
                span = span.reshape(groups + a_hi - a_lo, SUBLANES, lanes)
                for o in offs:
                    a = o // SUBLANES - a_lo
                    acc = acc + w_ref[o - first, :, c0:c0 + lanes] * span[a:a + groups]
            y_ref[pl.ds(r0, rows), c0:c0 + lanes] = acc.reshape(rows, lanes)
        return carry

    lax.fori_loop(0, tt // rows, row_chunk, 0)

    for r0 in range(0, tt, LN_ROWS):
        y = y_ref[r0:r0 + LN_ROWS, :] + b_ref[...]
        mu = jnp.mean(y, axis=-1, keepdims=True)
        var = jnp.mean(jnp.square(y - mu), axis=-1, keepdims=True)
        y = (y - mu) * lax.rsqrt(var + EPS) * g_ref[...] + beta_ref[...]
        o_ref[r0:r0 + LN_ROWS, :] = (y * _sigmoid(y)).astype(o_ref.dtype)


def _conv_module(u, conv_w, conv_b, ln_g, ln_b, batch, seq, tt=256, rows=128, lanes=128):
    ch = u.shape[1]
    conv_w = jnp.broadcast_to(conv_w[:, None, :], (CONV_WIDTH, SUBLANES, ch))
    nt = seq // tt
    halo_per_tile = tt // CONV_HALO
    vec = lambda a: a.reshape(1, ch)
    row_spec = pl.BlockSpec((1, ch), lambda b, i: (0, 0))
    return pl.pallas_call(
        functools.partial(_conv_kernel, tt=tt, rows=rows, lanes=lanes),
        out_shape=jax.ShapeDtypeStruct((batch * seq, ch), BF16),
        grid=(batch, nt),
        in_specs=[pl.BlockSpec((CONV_WIDTH, SUBLANES, ch), lambda b, i: (0, 0, 0)),
                  row_spec, row_spec, row_spec,
                  pl.BlockSpec((CONV_HALO, ch),
                               lambda b, i: (jnp.maximum((b * nt + i) * halo_per_tile - 1, 0), 0)),
                  pl.BlockSpec((tt, ch), lambda b, i: (b * nt + i, 0))],
        out_specs=pl.BlockSpec((tt, ch), lambda b, i: (b * nt + i, 0)),
        scratch_shapes=[pltpu.VMEM((SUBLANES, CONV_HALO + tt, ch), F32), pltpu.VMEM((tt, ch), F32)],
        compiler_params=_params(("parallel", "arbitrary")),
        name="conv_module",
    )(conv_w, vec(conv_b), vec(ln_g), vec(ln_b), u, u)


def _retention_kernel(lg_ref, g_ref, q_ref, k_ref, v_ref, gate_ref, o_ref, state_ref, *, chunk, seq, group):
    scale = RET_HEAD_DIM ** -0.5
    row = lax.broadcasted_iota(jnp.int32, (chunk, chunk), 0)
    col = lax.broadcasted_iota(jnp.int32, (chunk, chunk), 1)
    rel = (row - col).astype(F32)
    idx = lax.broadcasted_iota(jnp.int32, (chunk, 1), 0).astype(F32)
    heads = [slice(g * HEAD_W, (g + 1) * HEAD_W) for g in range(group)]
    decays = []
    for g in range(group):
        lg = lg_ref[g][:, 0:1]
        decays.append((jnp.where(rel >= 0, jnp.exp(lg * jnp.maximum(rel, 0.0)), 0.0) * scale,
                       jnp.exp(lg * (idx + 1.0)) * scale,
                       jnp.exp(lg * (chunk - 1.0 - idx)),
                       jnp.exp(lg * chunk)))
    state_ref[...] = jnp.zeros_like(state_ref)
    for n in range(seq // chunk):
        sl = slice(n * chunk, (n + 1) * chunk)
        for g in range(group):
            decay, q_decay, k_decay, chunk_decay = decays[g]
            q = q_ref[sl, heads[g]]
            k = k_ref[sl, heads[g]]
            v = v_ref[sl, heads[g]]
            scores = lax.dot_general(q, k, (((1,), (1,)), ((), ())), preferred_element_type=F32) * decay
            y = jnp.dot(scores.astype(BF16), v, preferred_element_type=F32)
            state = state_ref[g]
            y = y + jnp.dot((q.astype(F32) * q_decay).astype(BF16), state.astype(BF16),
                            preferred_element_type=F32)
            kd_t = (k.astype(F32) * k_decay).T.astype(BF16)
            state_ref[g] = state * chunk_decay + jnp.dot(kd_t, v, preferred_element_type=F32)
            mu = jnp.mean(y, axis=-1, keepdims=True)
            var = jnp.mean(jnp.square(y - mu), axis=-1, keepdims=True)
            y = (y - mu) * lax.rsqrt(var + EPS) * g_ref[:, heads[g]]
            o_ref[sl, heads[g]] = (gate_ref[sl, heads[g]] * y).astype(o_ref.dtype)


def _retention(qk, v, gate, gn_g, batch, seq, chunk=256, group=2):
    log_gamma = jnp.log1p(-jnp.exp2(-5.0 - jnp.arange(RET_HEADS, dtype=F32)))
    lg = jnp.broadcast_to(log_gamma[:, None, None], (RET_HEADS, 1, LANES))
    n_groups = RET_HEADS // group
    gw = group * HEAD_W
    head = lambda b, h: (b, h)
    return pl.pallas_call(
        functools.partial(_retention_kernel, chunk=chunk, seq=seq, group=group),
        out_shape=jax.ShapeDtypeStruct((batch * seq, RET_HEADS * HEAD_W), BF16),
        grid=(batch, n_groups),
        in_specs=[pl.BlockSpec((group, 1, LANES), lambda b, h: (h, 0, 0)),
                  pl.BlockSpec((1, gw), lambda b, h: (0, h)),
                  pl.BlockSpec((seq, gw), head),
                  pl.BlockSpec((seq, gw), lambda b, h: (b, n_groups + h)),
                  pl.BlockSpec((seq, gw), head),
                  pl.BlockSpec((seq, gw), head)],
        out_specs=pl.BlockSpec((seq, gw), head),
        scratch_shapes=[pltpu.VMEM((group, RET_HEAD_DIM, RET_HEAD_DIM), F32)],
        compiler_params=_params(("parallel", "parallel")),
        name="retention",
    )(lg, gn_g.reshape(1, -1), qk, qk, v, gate)


def _merge_out_kernel(ya_ref, yb_ref, yr_ref, wbr_ref, g_ref, wo_ref, r_ref, gain_ref,
                      x_ref, xb_ref, inv_ref):
    d = x_ref.shape[1]
    merged = None
    for b, y_ref in enumerate((ya_ref, yb_ref, yr_ref)):
        term = g_ref[:, b * d:(b + 1) * d].astype(F32) * jnp.dot(y_ref[...], wbr_ref[b],
                                                                preferred_element_type=F32)
        merged = term if merged is None else merged + term
    x = r_ref[...] + jnp.dot(merged.astype(BF16), wo_ref[...], preferred_element_type=F32)
    _emit_rows(x, gain_ref, (x_ref, xb_ref, inv_ref), final=False)


def _merge_out(ya, yb, yr, w_branch, gates, w_out, res, g_next, tm=TM_ROWS_WIDE):
    m, kdim = ya.shape
    d = res.shape[1]
    rows = lambda i: (i, 0)
    once = pl.Buffered(1)
    return pl.pallas_call(
        _merge_out_kernel,
        out_shape=(jax.ShapeDtypeStruct((m, d), F32), jax.ShapeDtypeStruct((m, d), BF16),
                   jax.ShapeDtypeStruct((m, LANES), F32)),
        grid=(m // tm,),
        in_specs=[pl.BlockSpec((tm, kdim), rows), pl.BlockSpec((tm, kdim), rows),
                  pl.BlockSpec((tm, kdim), rows),
                  pl.BlockSpec((N_BRANCH, kdim, d), lambda i: (0, 0, 0), pipeline_mode=once),
                  pl.BlockSpec((tm, N_BRANCH * d), rows),
                  pl.BlockSpec((d, d), lambda i: (0, 0), pipeline_mode=once),
                  pl.BlockSpec((tm, d), rows),
                  pl.BlockSpec((1, d), lambda i: (0, 0))],
        out_specs=(pl.BlockSpec((tm, d), rows), pl.BlockSpec((tm, d), rows),
                   pl.BlockSpec((tm, LANES), rows)),
        compiler_params=_params(("parallel",)),
        name="merge_out",
    )(ya, yb, yr, w_branch, gates, w_out, res, g_next.reshape(1, d))


def _xattn_kernel(q_ref, k_ref, v_ref, o_ref, *, scale, hd):
    heads = [slice(h * hd, (h + 1) * hd) for h in range(XATTN_HEADS)]
    logits = [lax.dot_general(q_ref[:, hs], k_ref[:, hs], (((1,), (1,)), ((), ())),
                              preferred_element_type=F32) for hs in heads]
    for hs, s in zip(heads, logits):
        s = s * scale
        p = jnp.exp(s - jnp.max(s, axis=-1, keepdims=True))
        l = jnp.sum(p, axis=-1, keepdims=True)
        o = jnp.dot(p.astype(BF16), v_ref[:, hs], preferred_element_type=F32) / l
        o_ref[:, hs] = o.astype(o_ref.dtype)


def _cross_attention(q, kv, batch, seq, mem_tokens, tq=512):
    d = q.shape[1]
    hd = d // XATTN_HEADS
    nq = seq // tq
    return pl.pallas_call(
        functools.partial(_xattn_kernel, scale=hd ** -0.5, hd=hd),
        out_shape=jax.ShapeDtypeStruct((batch * seq, d), BF16),
        grid=(batch, nq),
        in_specs=[pl.BlockSpec((tq, d), lambda b, i: (b * nq + i, 0)),
                  pl.BlockSpec((mem_tokens, d), lambda b, i: (b, 0)),
                  pl.BlockSpec((mem_tokens, d), lambda b, i: (b, 1))],
        out_specs=pl.BlockSpec((tq, d), lambda b, i: (b * nq + i, 0)),
        compiler_params=_params(("parallel", "arbitrary")),
        name="cross_attention",
    )(q, kv, kv)


def _rope_tables(seq, half, inv_freq):
    ang = jnp.arange(seq, dtype=F32)[:, None] * inv_freq[None, :]
    cos, sin = jnp.cos(ang), jnp.sin(ang)
    reps = LANES // (2 * half)
    cos_l = jnp.tile(jnp.concatenate([cos, cos], axis=1), (1, reps))
    sin_l = jnp.tile(jnp.concatenate([-sin, sin], axis=1), (1, reps))
    return cos_l, sin_l


def kernel(x, mem, norm_mix, w_in, diff_lambda, diff_subln, conv_w, conv_b, conv_ln_g, conv_ln_b,
           ret_gn_g, w_branch, w_out, norm_xattn, norm_mem, xattn_wq, xattn_wkv, xattn_wo,
           norm_ffn, ffn_w13, ffn_w2, norm_final):
    batch, seq, d = x.shape
    mem_tokens = mem.shape[1]
    depth = w_in.shape[0]
    mix_w = d // 2
    ffn_hidden = ffn_w2.shape[1]
    tm, tn, tn_glu = TM, TN, TN_GATED
    rows_per_seq = seq // tm

    diff_inv = ROPE_THETA ** (-jnp.arange(0, DIFF_HEAD_DIM, 2, dtype=F32) / DIFF_HEAD_DIM)
    diff_tabs = _rope_tables(seq, DIFF_HEAD_DIM // 2, diff_inv)
    ret_inv = 1.0 / (ROPE_THETA ** jnp.linspace(0.0, 1.0, RET_HEAD_DIM // 2, dtype=F32))
    ret_tabs = _rope_tables(seq, RET_HEAD_DIM // 2, ret_inv)
    tab_spec = pl.BlockSpec((tm, LANES), lambda j, i: (i % rows_per_seq, 0))

    xf = x.reshape(batch * seq, d)
    memf = mem.reshape(batch * mem_tokens, d)
    seg = {"dq": 0, "dv": 2 * mix_w, "ca": 3 * mix_w, "cb": 4 * mix_w, "rq": 5 * mix_w, "rv": 7 * mix_w,
           "rg": 8 * mix_w, "gates": 9 * mix_w}

    def off(name, width=tn):
        return lambda j, o=seg[name] // width: o + j

    ident = lambda j: j

    xb, inv = _prep_norm(xf, norm_mix[0])

    for i in range(depth):
        lambda_init = 0.8 - 0.6 * math.exp(-0.3 * i)
        last = i == depth - 1

        dqk = _matmul(xb, w_in, i, [off("dq")], 2 * mix_w,
                      functools.partial(_epi_rope, half=DIFF_HEAD_DIM // 2), BF16, inv=inv,
                      aux=diff_tabs, aux_specs=(tab_spec, tab_spec), tm=tm, tn=tn)
        rqk = _matmul(xb, w_in, i, [off("rq")], 2 * mix_w,
                      functools.partial(_epi_rope, half=RET_HEAD_DIM // 2), BF16, inv=inv,
                      aux=ret_tabs, aux_specs=(tab_spec, tab_spec), tm=tm, tn=tn)
        dv = _matmul(xb, w_in, i, [off("dv")], mix_w, _epi_cast, BF16, inv=inv, tm=tm, tn=tn)
        rv = _matmul(xb, w_in, i, [off("rv")], mix_w, _epi_cast, BF16, inv=inv, tm=tm, tn=tn)
        u = _matmul(xb, w_in, i, [off("ca", tn_glu), off("cb", tn_glu)], mix_w, _epi_glu, F32, inv=inv,
                    tm=tm, tn=tn_glu)
        rgate = _matmul(xb, w_in, i, [off("rg")], mix_w, _epi_silu, F32, inv=inv, tm=tm, tn=tn)
        gates = _matmul(xb, w_in, i, [off("gates")], N_BRANCH * d, _epi_sigmoid, BF16, inv=inv,
                        tm=tm, tn=tn)

        ya = _diff_attention(dqk, dv, diff_lambda[i], diff_subln[i], batch, seq, lambda_init)
        yb = _conv_module(u, conv_w[i], conv_b[i], conv_ln_g[i], conv_ln_b[i], batch, seq)
        yr = _retention(rqk, rv, rgate, ret_gn_g[i], batch, seq)

        w_br = _cast_layer(w_branch.reshape(depth, N_BRANCH * mix_w, d), i).reshape(N_BRANCH, mix_w, d)
        xf, xb, inv = _merge_out(ya, yb, yr, w_br, gates, _cast_layer(w_out, i), xf, norm_xattn[i])

        hm = _rmsnorm(memf, norm_mem[i], BF16, tm=256)
        q = _matmul(xb, xattn_wq, i, [ident], d, _epi_cast, BF16, inv=inv, tm=tm, tn=tn)
        kv = _matmul(hm, xattn_wkv, i, [ident], 2 * d, _epi_cast, BF16,
                     tm=min(tm, batch * mem_tokens), tn=tn)
        o = _cross_attention(q, kv, batch, seq, mem_tokens)
        xf, xb, inv = _matmul_whole_rows(o, _cast_layer(xattn_wo, i), xf, norm_ffn[i], False,
                                         tm=TM_ROWS_D)

        act = _matmul(xb, ffn_w13, i, [ident, lambda j, o=ffn_hidden // tn_glu: o + j], ffn_hidden,
                      _epi_swiglu, BF16, inv=inv, tm=tm, tn=tn_glu)
        w2 = _cast_layer(ffn_w2, i)
        if last:
            (out,) = _matmul_whole_rows(act, w2, xf, norm_final, True, tm=TM_ROWS_WIDE)
        else:
            xf, xb, inv = _matmul_whole_rows(act, w2, xf, norm_mix[i + 1], False, tm=TM_ROWS_WIDE)

    return out.reshape(batch, seq, d)
```

```python
import functools
import math

import jax
import jax.numpy as jnp
from jax import lax
from jax.experimental import pallas as pl
from jax.experimental.pallas import tpu as pltpu

F32 = jnp.float32
BF16 = jnp.bfloat16

DIFF_HEADS = 8
DIFF_HEAD_DIM = 64
RET_HEADS = 8
RET_HEAD_DIM = 128
HEAD_W = 128
CONV_WIDTH = 31
N_BRANCH = 3
XATTN_HEADS = 4
ROPE_THETA = 10000.0
EPS = 1e-6
NEG_INF = -1e30

V7X_VMEM_BYTES = 64 * 1024 * 1024
VMEM_LIMIT_BYTES = V7X_VMEM_BYTES - 12 * 1024 * 1024
LANES = 128
SUBLANES = 8

TM = 1024
TN = 1024
TN_GATED = 512
TM_ROWS_D = 512
TM_ROWS_WIDE = 256


def _params(semantics):
    return pltpu.CompilerParams(dimension_semantics=semantics, vmem_limit_bytes=VMEM_LIMIT_BYTES)


def _rmsnorm_kernel(x_ref, g_ref, o_ref):
    x = x_ref[...]
    y = x * lax.rsqrt(jnp.mean(x * x, axis=-1, keepdims=True) + EPS)
    o_ref[...] = (y * g_ref[...]).astype(o_ref.dtype)


def _rmsnorm(x, g, out_dtype, tm=512):
    m, d = x.shape
    return pl.pallas_call(
        _rmsnorm_kernel,
        out_shape=jax.ShapeDtypeStruct((m, d), out_dtype),
        grid=(m // tm,),
        in_specs=[pl.BlockSpec((tm, d), lambda i: (i, 0)),
                  pl.BlockSpec((1, d), lambda i: (0, 0))],
        out_specs=pl.BlockSpec((tm, d), lambda i: (i, 0)),
        compiler_params=_params(("parallel",)),
        name="rmsnorm",
    )(x, g.reshape(1, d))


def _inv_rms_lanes(sum_sq, d):
    return jnp.broadcast_to(lax.rsqrt(sum_sq * (1.0 / d) + EPS), (sum_sq.shape[0], LANES))


def _prep_norm_kernel(x_ref, g_ref, xb_ref, inv_ref):
    x = x_ref[...]
    xb_ref[...] = (x * g_ref[...]).astype(xb_ref.dtype)
    inv_ref[...] = _inv_rms_lanes(jnp.sum(x * x, axis=-1, keepdims=True), x.shape[1])


def _prep_norm(x, g, tm=512):
    m, d = x.shape
    return pl.pallas_call(
        _prep_norm_kernel,
        out_shape=(jax.ShapeDtypeStruct((m, d), BF16), jax.ShapeDtypeStruct((m, LANES), F32)),
        grid=(m // tm,),
        in_specs=[pl.BlockSpec((tm, d), lambda i: (i, 0)),
                  pl.BlockSpec((1, d), lambda i: (0, 0))],
        out_specs=(pl.BlockSpec((tm, d), lambda i: (i, 0)), pl.BlockSpec((tm, LANES), lambda i: (i, 0))),
        compiler_params=_params(("parallel",)),
        name="prep_norm",
    )(x, g.reshape(1, d))


def _cast_kernel(w_ref, o_ref):
    o_ref[...] = w_ref[0].astype(o_ref.dtype)


def _cast_layer(w, layer, rows=512):
    _, r, n = w.shape
    return pl.pallas_call(
        _cast_kernel,
        out_shape=jax.ShapeDtypeStruct((r, n), BF16),
        grid=(r // rows,),
        in_specs=[pl.BlockSpec((1, rows, n), lambda i: (layer, i, 0))],
        out_specs=pl.BlockSpec((rows, n), lambda i: (i, 0)),
        compiler_params=_params(("parallel",)),
        name="cast_weights",
    )(w)


def _epi_cast(accs, aux):
    return accs[0]


def _sigmoid(x):
    return 0.5 * jnp.tanh(0.5 * x) + 0.5


def _epi_sigmoid(accs, aux):
    return _sigmoid(accs[0])


def _epi_silu(accs, aux):
    return accs[0] * _sigmoid(accs[0])


def _epi_glu(accs, aux):
    return accs[0] * _sigmoid(accs[1])


def _epi_swiglu(accs, aux):
    return accs[0] * _sigmoid(accs[0]) * accs[1]


def _epi_rope(accs, aux, *, half):
    acc = accs[0]
    cos, sin_signed = aux[0][...], aux[1][...]
    tm, tn = acc.shape
    lane = lax.broadcasted_iota(jnp.int32, (tm, LANES), 1)
    first_half = (lane % (2 * half)) < half
    outs = []
    for c in range(tn // LANES):
        x = acc[:, c * LANES:(c + 1) * LANES]
        if 2 * half == LANES:
            partner = pltpu.roll(x, half, 1)
        else:
            partner = jnp.where(first_half, pltpu.roll(x, LANES - half, 1), pltpu.roll(x, half, 1))
        outs.append(x * cos + partner * sin_signed)
    return jnp.concatenate(outs, axis=1)


def _cast_weights_once(w_refs, wb_refs):
    @pl.when(pl.program_id(1) == 0)
    def _():
        for w, wb in zip(w_refs, wb_refs):
            wb[...] = w[0].astype(BF16)


def _mm_kernel(*refs, n_w, epilogue, row_scaled):
    a_ref = refs[0]
    w_refs = refs[1:1 + n_w]
    aux_refs = refs[1 + n_w:-1 - n_w]
    o_ref = refs[-1 - n_w]
    wb_refs = refs[-n_w:]
    _cast_weights_once(w_refs, wb_refs)
    a = a_ref[...]
    accs = [jnp.dot(a, wb[...], preferred_element_type=F32) for wb in wb_refs]
    if row_scaled:
        inv = jnp.tile(aux_refs[0][...], (1, o_ref.shape[1] // LANES))
        accs = [acc * inv for acc in accs]
        aux_refs = aux_refs[1:]
    o_ref[...] = epilogue(accs, aux_refs).astype(o_ref.dtype)


def _matmul(a, w, layer, col_blocks, n_out, epilogue, out_dtype, inv=None, aux=(), aux_specs=(),
            tm=1024, tn=1024):
    m, k = a.shape
    n_w = len(col_blocks)
    in_specs = [pl.BlockSpec((tm, k), lambda j, i: (i, 0))]
    for cb in col_blocks:
        in_specs.append(pl.BlockSpec((1, k, tn), lambda j, i, cb=cb: (layer, 0, cb(j))))
    if inv is not None:
        in_specs.append(pl.BlockSpec((tm, LANES), lambda j, i: (i, 0)))
        aux = (inv,) + tuple(aux)
    in_specs.extend(aux_specs)
    return pl.pallas_call(
        functools.partial(_mm_kernel, n_w=n_w, epilogue=epilogue, row_scaled=inv is not None),
        out_shape=jax.ShapeDtypeStruct((m, n_out), out_dtype),
        grid=(n_out // tn, m // tm),
        in_specs=in_specs,
        out_specs=pl.BlockSpec((tm, tn), lambda j, i: (i, j)),
        scratch_shapes=[pltpu.VMEM((k, tn), BF16) for _ in range(n_w)],
        compiler_params=_params(("arbitrary", "arbitrary")),
        name="matmul_" + getattr(epilogue, "__name__", "rope"),
    )(a, *([w] * n_w), *aux)


def _emit_rows(x, gain_ref, out_refs, final):
    d = x.shape[1]
    sum_sq = jnp.sum(x * x, axis=-1, keepdims=True)
    if final:
        out_refs[0][...] = x * lax.rsqrt(sum_sq * (1.0 / d) + EPS) * gain_ref[...]
    else:
        out_refs[0][...] = x
        out_refs[1][...] = (x * gain_ref[...]).astype(BF16)
        out_refs[2][...] = _inv_rms_lanes(sum_sq, d)


def _rows_out(m, d, tm, final):
    rows = lambda i: (i, 0)
    shapes = [jax.ShapeDtypeStruct((m, d), F32)]
    specs = [pl.BlockSpec((tm, d), rows)]
    if not final:
        shapes += [jax.ShapeDtypeStruct((m, d), BF16), jax.ShapeDtypeStruct((m, LANES), F32)]
        specs += [pl.BlockSpec((tm, d), rows), pl.BlockSpec((tm, LANES), rows)]
    return tuple(shapes), tuple(specs)


def _mm_whole_rows_kernel(a_ref, w_ref, r_ref, gain_ref, *out_refs, final):
    x = r_ref[...] + jnp.dot(a_ref[...], w_ref[...], preferred_element_type=F32)
    _emit_rows(x, gain_ref, out_refs, final)


def _matmul_whole_rows(a, w, res, gain, final, tm):
    m, k = a.shape
    d = w.shape[1]
    rows = lambda i: (i, 0)
    shapes, specs = _rows_out(m, d, tm, final)
    return pl.pallas_call(
        functools.partial(_mm_whole_rows_kernel, final=final),
        out_shape=shapes,
        grid=(m // tm,),
        in_specs=[pl.BlockSpec((tm, k), rows),
                  pl.BlockSpec((k, d), lambda i: (0, 0), pipeline_mode=pl.Buffered(1)),
                  pl.BlockSpec((tm, d), rows),
                  pl.BlockSpec((1, d), lambda i: (0, 0))],
        out_specs=specs,
        compiler_params=_params(("parallel",)),
        name="matmul_whole_rows",
    )(a, w, res, gain.reshape(1, d))


ONES_ROWS = 16


def _diff_attn_kernel(lam_ref, sub_ref, q_ref, k_ref, v_ref, o_ref, vt_ref, acc_ref, *, tq, group,
                      lambda_init):
    qi = pl.program_id(2)
    tk = tq
    nkv = v_ref.shape[0] // tk
    heads = [slice(g * HEAD_W, (g + 1) * HEAD_W) for g in range(group)]

    @pl.when(qi == 0)
    def _():
        for g in range(group):
            for jb in range(nkv):
                vt_ref[g, jb, 0:HEAD_W, :] = v_ref[jb * tk:(jb + 1) * tk, heads[g]].astype(F32).T.astype(BF16)
                row = lax.broadcasted_iota(jnp.int32, (ONES_ROWS, tk), 0)
                vt_ref[g, jb, HEAD_W:HEAD_W + ONES_ROWS, :] = jnp.where(row == 0, 1.0, 0.0).astype(BF16)

    scale = DIFF_HEAD_DIM ** -0.5 * math.log2(math.e)
    dim = lax.broadcasted_iota(jnp.int32, (HEAD_W, tq), 0)
    qsts = []
    for g in range(group):
        qt = (q_ref[:, heads[g]].astype(F32) * scale).T
        qsts.append(jnp.concatenate([jnp.where(dim < DIFF_HEAD_DIM, qt, 0.0),
                                     jnp.where(dim >= DIFF_HEAD_DIM, qt, 0.0)], axis=1).astype(BF16))

    def block(j, carry, masked):
        new = []
        logits = [jnp.dot(k_ref[pl.ds(pl.multiple_of(j * tk, tk), tk), heads[g]], qsts[g],
                          preferred_element_type=F32) for g in range(group)]
        for g in range(group):
            m_prev = carry[g]
            s = logits[g]
            if masked:
                key = lax.broadcasted_iota(jnp.int32, (tk, 2 * tq), 0)
                qry = lax.broadcasted_iota(jnp.int32, (tk, 2 * tq), 1)
                qry = jnp.where(qry >= tq, qry - tq, qry)
                s = jnp.where(key <= qry, s, NEG_INF)
            sb = s.astype(BF16)
            m_new = jnp.maximum(m_prev, jnp.max(sb, axis=0, keepdims=True))
            alpha = jnp.exp2(m_prev.astype(F32) - m_new.astype(F32))
            p = jnp.exp2(sb - m_new)
            acc_ref[g] = alpha * acc_ref[g] + jnp.dot(vt_ref[g, j], p, preferred_element_type=F32)
            new.append(m_new)
        return tuple(new)

    acc_ref[...] = jnp.zeros_like(acc_ref)
    init = tuple(jnp.full((1, 2 * tq), NEG_INF, BF16) for _ in range(group))
    carry = lax.fori_loop(0, qi, functools.partial(block, masked=False), init)
    block(qi, carry, masked=True)

    lam_p = lam_ref[...]
    lam = (jnp.exp(jnp.sum(lam_p[0:1] * lam_p[1:2], axis=-1, keepdims=True))
           - jnp.exp(jnp.sum(lam_p[2:3] * lam_p[3:4], axis=-1, keepdims=True)) + lambda_init)
    for g in range(group):
        o = acc_ref[g, 0:HEAD_W, :] * (1.0 / acc_ref[g, HEAD_W:HEAD_W + 1, :])
        y = o[:, :tq] - lam * o[:, tq:]
        y = y * lax.rsqrt(jnp.mean(y * y, axis=0, keepdims=True) + EPS) * sub_ref[...]
        o_ref[:, heads[g]] = (y * (1.0 - lambda_init)).T.astype(o_ref.dtype)


def _diff_attention(qk, v, diff_lambda, diff_subln, batch, seq, lambda_init, tq=256, group=8):
    nq = seq // tq
    n_groups = DIFF_HEADS // group
    gw = group * HEAD_W
    return pl.pallas_call(
        functools.partial(_diff_attn_kernel, tq=tq, group=group, lambda_init=lambda_init),
        out_shape=jax.ShapeDtypeStruct((batch * seq, DIFF_HEADS * HEAD_W), BF16),
        grid=(batch, n_groups, nq),
        in_specs=[pl.BlockSpec((4, DIFF_HEAD_DIM), lambda b, h, i: (0, 0)),
                  pl.BlockSpec((HEAD_W, 1), lambda b, h, i: (0, 0)),
                  pl.BlockSpec((tq, gw), lambda b, h, i: (b * nq + i, h)),
                  pl.BlockSpec((seq, gw), lambda b, h, i: (b, n_groups + h)),
                  pl.BlockSpec((seq, gw), lambda b, h, i: (b, h))],
        out_specs=pl.BlockSpec((tq, gw), lambda b, h, i: (b * nq + i, h)),
        scratch_shapes=[pltpu.VMEM((group, nq, HEAD_W + ONES_ROWS, tq), BF16),
                        pltpu.VMEM((group, HEAD_W + ONES_ROWS, 2 * tq), F32)],
        compiler_params=_params(("parallel", "parallel", "arbitrary")),
        name="diff_attention",
    )(diff_lambda, diff_subln.reshape(HEAD_W, 1), qk, qk, v)


CONV_HALO = 32
LN_ROWS = 64


def _conv_kernel(w_ref, b_ref, g_ref, beta_ref, prev_ref, cur_ref, o_ref, sh_ref, y_ref, *, tt, rows,
                 lanes):
    i = pl.program_id(1)
    ch = sh_ref.shape[2]
    n_ext = CONV_HALO + tt

    @pl.when(i == 0)
    def _():
        sh_ref[0, 0:CONV_HALO, :] = jnp.zeros((CONV_HALO, ch), F32)

    @pl.when(i > 0)
    def _():
        sh_ref[0, 0:CONV_HALO, :] = prev_ref[...]

    sh_ref[0, CONV_HALO:n_ext, :] = cur_ref[...]
    for c0 in range(0, ch, LANES):
        ext = sh_ref[0, :, c0:c0 + LANES]
        for s in range(1, SUBLANES):
            sh_ref[s, 0:n_ext - SUBLANES, c0:c0 + LANES] = pltpu.roll(ext, n_ext - s, 0)[0:n_ext - SUBLANES]

    first = CONV_HALO - (CONV_WIDTH - 1)
    groups = rows // SUBLANES

    def row_chunk(c, carry):
        r0 = pl.multiple_of(c * rows, rows)
        for c0 in range(0, ch, lanes):
            acc = jnp.zeros((groups, SUBLANES, lanes), F32)
            for s in range(SUBLANES):
                offs = [o for o in range(first, first + CONV_WIDTH) if o % SUBLANES == s]
                a_lo, a_hi = offs[0] // SUBLANES, offs[-1] // SUBLANES
                span = sh_ref[s, pl.ds(r0 + SUBLANES * a_lo, rows + SUBLANES * (a_hi - a_lo)), c0:c0 + lanes]
                span = span.reshape(groups + a_hi - a_lo, SUBLANES, lanes)
                for o in offs:
                    a = o // SUBLANES - a_lo
                    acc = acc + w_ref[o - first, :, c0:c0 + lanes] * span[a:a + groups]
            y_ref[pl.ds(r0, rows), c0:c0 + lanes] = acc.reshape(rows, lanes)
        return carry

    lax.fori_loop(0, tt // rows, row_chunk, 0)

    for r0 in range(0, tt, LN_ROWS):
        y = y_ref[r0:r0 + LN_ROWS, :] + b_ref[...]
        mu = jnp.mean(y, axis=-1, keepdims=True)
        var = jnp.mean(jnp.square(y - mu), axis=-1, keepdims=True)
        y = (y - mu) * lax.rsqrt(var + EPS) * g_ref[...] + beta_ref[...]
        o_ref[r0:r0 + LN_ROWS, :] = (y * _sigmoid(y)).astype(o_ref.dtype)


def _conv_module(u, conv_w, conv_b, ln_g, ln_b, batch, seq, tt=256, rows=128, lanes=128):
    ch = u.shape[1]
    conv_w = jnp.broadcast_to(conv_w[:, None, :], (CONV_WIDTH, SUBLANES, ch))
    nt = seq // tt
    halo_per_tile = tt // CONV_HALO
    vec = lambda a: a.reshape(1, ch)
    row_spec = pl.BlockSpec((1, ch), lambda b, i: (0, 0))
    return pl.pallas_call(
        functools.partial(_conv_kernel, tt=tt, rows=rows, lanes=lanes),
        out_shape=jax.ShapeDtypeStruct((batch * seq, ch), BF16),
        grid=(batch, nt),
        in_specs=[pl.BlockSpec((CONV_WIDTH, SUBLANES, ch), lambda b, i: (0, 0, 0)),
                  row_spec, row_spec, row_spec,
                  pl.BlockSpec((CONV_HALO, ch),
                               lambda b, i: (jnp.maximum((b * nt + i) * halo_per_tile - 1, 0), 0)),
                  pl.BlockSpec((tt, ch), lambda b, i: (b * nt + i, 0))],
        out_specs=pl.BlockSpec((tt, ch), lambda b, i: (b * nt + i, 0)),
        scratch_shapes=[pltpu.VMEM((SUBLANES, CONV_HALO + tt, ch), F32), pltpu.VMEM((tt, ch), F32)],
        compiler_params=_params(("parallel", "arbitrary")),
        name="conv_module",
    )(conv_w, vec(conv_b), vec(ln_g), vec(ln_b), u, u)


def _retention_kernel(lg_ref, g_ref, q_ref, k_ref, v_ref, gate_ref, o_ref, state_ref, *, chunk, seq, group):
    scale = RET_HEAD_DIM ** -0.5
    row = lax.broadcasted_iota(jnp.int32, (chunk, chunk), 0)
    col = lax.broadcasted_iota(jnp.int32, (chunk, chunk), 1)
    rel = (row - col).astype(F32)
    idx = lax.broadcasted_iota(jnp.int32, (chunk, 1), 0).astype(F32)
    heads = [slice(g * HEAD_W, (g + 1) * HEAD_W) for g in range(group)]
    decays = []
    for g in range(group):
        lg = lg_ref[g][:, 0:1]
        decays.append((jnp.where(rel >= 0, jnp.exp(lg * jnp.maximum(rel, 0.0)), 0.0) * scale,
                       jnp.exp(lg * (idx + 1.0)) * scale,
                       jnp.exp(lg * (chunk - 1.0 - idx)),
                       jnp.exp(lg * chunk)))
    state_ref[...] = jnp.zeros_like(state_ref)
    for n in range(seq // chunk):
        sl = slice(n * chunk, (n + 1) * chunk)
        for g in range(group):
            decay, q_decay, k_decay, chunk_decay = decays[g]
            q = q_ref[sl, heads[g]]
            k = k_ref[sl, heads[g]]
            v = v_ref[sl, heads[g]]
            scores = lax.dot_general(q, k, (((1,), (1,)), ((), ())), preferred_element_type=F32) * decay
            y = jnp.dot(scores.astype(BF16), v, preferred_element_type=F32)
            state = state_ref[g]
            y = y + jnp.dot((q.astype(F32) * q_decay).astype(BF16), state.astype(BF16),
                            preferred_element_type=F32)
            kd_t = (k.astype(F32) * k_decay).T.astype(BF16)
            state_ref[g] = state * chunk_decay + jnp.dot(kd_t, v, preferred_element_type=F32)
            mu = jnp.mean(y, axis=-1, keepdims=True)
            var = jnp.mean(jnp.square(y - mu), axis=-1, keepdims=True)
            y = (y - mu) * lax.rsqrt(var + EPS) * g_ref[:, heads[g]]
            o_ref[sl, heads[g]] = (gate_ref[sl, heads[g]] * y).astype(o_ref.dtype)


def _retention(qk, v, gate, gn_g, batch, seq, chunk=256, group=2):
    log_gamma = jnp.log1p(-jnp.exp2(-5.0 - jnp.arange(RET_HEADS, dtype=F32)))
    lg = jnp.broadcast_to(log_gamma[:, None, None], (RET_HEADS, 1, LANES))
    n_groups = RET_HEADS // group
    gw = group * HEAD_W
    head = lambda b, h: (b, h)
    return pl.pallas_call(
        functools.partial(_retention_kernel, chunk=chunk, seq=seq, group=group),
        out_shape=jax.ShapeDtypeStruct((batch * seq, RET_HEADS * HEAD_W), BF16),
        grid=(batch, n_groups),
        in_specs=[pl.BlockSpec((group, 1, LANES), lambda b, h: (h, 0, 0)),
                  pl.BlockSpec((1, gw), lambda b, h: (0, h)),
                  pl.BlockSpec((seq, gw), head),
                  pl.BlockSpec((seq, gw), lambda b, h: (b, n_groups + h)),
                  pl.BlockSpec((seq, gw), head),
                  pl.BlockSpec((seq, gw), head)],
        out_specs=pl.BlockSpec((seq, gw), head),
        scratch_shapes=[pltpu.VMEM((group, RET_HEAD_DIM, RET_HEAD_DIM), F32)],
        compiler_params=_params(("parallel", "parallel")),
        name="retention",
    )(lg, gn_g.reshape(1, -1), qk, qk, v, gate)


def _merge_out_kernel(ya_ref, yb_ref, yr_ref, wbr_ref, g_ref, wo_ref, r_ref, gain_ref,
                      x_ref, xb_ref, inv_ref):
    d = x_ref.shape[1]
    merged = None
    for b, y_ref in enumerate((ya_ref, yb_ref, yr_ref)):
        term = g_ref[:, b * d:(b + 1) * d].astype(F32) * jnp.dot(y_ref[...], wbr_ref[b],
                                                                preferred_element_type=F32)
        merged = term if merged is None else merged + term
    x = r_ref[...] + jnp.dot(merged.astype(BF16), wo_ref[...], preferred_element_type=F32)
    _emit_rows(x, gain_ref, (x_ref, xb_ref, inv_ref), final=False)


def _merge_out(ya, yb, yr, w_branch, gates, w_out, res, g_next, tm=TM_ROWS_WIDE):
    m, kdim = ya.shape
    d = res.shape[1]
    rows = lambda i: (i, 0)
    once = pl.Buffered(1)
    return pl.pallas_call(
        _merge_out_kernel,
        out_shape=(jax.ShapeDtypeStruct((m, d), F32), jax.ShapeDtypeStruct((m, d), BF16),
                   jax.ShapeDtypeStruct((m, LANES), F32)),
        grid=(m // tm,),
        in_specs=[pl.BlockSpec((tm, kdim), rows), pl.BlockSpec((tm, kdim), rows),
                  pl.BlockSpec((tm, kdim), rows),
                  pl.BlockSpec((N_BRANCH, kdim, d), lambda i: (0, 0, 0), pipeline_mode=once),
                  pl.BlockSpec((tm, N_BRANCH * d), rows),
                  pl.BlockSpec((d, d), lambda i: (0, 0), pipeline_mode=once),
                  pl.BlockSpec((tm, d), rows),
                  pl.BlockSpec((1, d), lambda i: (0, 0))],
        out_specs=(pl.BlockSpec((tm, d), rows), pl.BlockSpec((tm, d), rows),
                   pl.BlockSpec((tm, LANES), rows)),
        compiler_params=_params(("parallel",)),
        name="merge_out",
    )(ya, yb, yr, w_branch, gates, w_out, res, g_next.reshape(1, d))


def _xattn_kernel(q_ref, k_ref, v_ref, o_ref, *, scale, hd):
    heads = [slice(h * hd, (h + 1) * hd) for h in range(XATTN_HEADS)]
    logits = [lax.dot_general(q_ref[:, hs], k_ref[:, hs], (((1,), (1,)), ((), ())),
                              preferred_element_type=F32) for hs in heads]
    for hs, s in zip(heads, logits):
        s = s * scale
        p = jnp.exp(s - jnp.max(s, axis=-1, keepdims=True))
        l = jnp.sum(p, axis=-1, keepdims=True)
        o = jnp.dot(p.astype(BF16), v_ref[:, hs], preferred_element_type=F32) / l
        o_ref[:, hs] = o.astype(o_ref.dtype)


def _cross_attention(q, kv, batch, seq, mem_tokens, tq=512):
    d = q.shape[1]
    hd = d // XATTN_HEADS
    nq = seq // tq
    return pl.pallas_call(
        functools.partial(_xattn_kernel, scale=hd ** -0.5, hd=hd),
        out_shape=jax.ShapeDtypeStruct((batch * seq, d), BF16),
        grid=(batch, nq),
        in_specs=[pl.BlockSpec((tq, d), lambda b, i: (b * nq + i, 0)),
                  pl.BlockSpec((mem_tokens, d), lambda b, i: (b, 0)),
                  pl.BlockSpec((mem_tokens, d), lambda b, i: (b, 1))],
        out_specs=pl.BlockSpec((tq, d), lambda b, i: (b * nq + i, 0)),
        compiler_params=_params(("parallel", "arbitrary")),
        name="cross_attention",
    )(q, kv, kv)


def _rope_tables(seq, half, inv_freq):
    ang = jnp.arange(seq, dtype=F32)[:, None] * inv_freq[None, :]
    cos, sin = jnp.cos(ang), jnp.sin(ang)
    reps = LANES // (2 * half)
    cos_l = jnp.tile(jnp.concatenate([cos, cos], axis=1), (1, reps))
    sin_l = jnp.tile(jnp.concatenate([-sin, sin], axis=1), (1, reps))
    return cos_l, sin_l


def kernel(x, mem, norm_mix, w_in, diff_lambda, diff_subln, conv_w, conv_b, conv_ln_g, conv_ln_b,
           ret_gn_g, w_branch, w_out, norm_xattn, norm_mem, xattn_wq, xattn_wkv, xattn_wo,
           norm_ffn, ffn_w13, ffn_w2, norm_final):
    batch, seq, d = x.shape
    mem_tokens = mem.shape[1]
    depth = w_in.shape[0]
    mix_w = d // 2
    ffn_hidden = ffn_w2.shape[1]
    tm, tn, tn_glu = TM, TN, TN_GATED
    rows_per_seq = seq // tm

    diff_inv = ROPE_THETA ** (-jnp.arange(0, DIFF_HEAD_DIM, 2, dtype=F32) / DIFF_HEAD_DIM)
    diff_tabs = _rope_tables(seq, DIFF_HEAD_DIM // 2, diff_inv)
    ret_inv = 1.0 / (ROPE_THETA ** jnp.linspace(0.0, 1.0, RET_HEAD_DIM // 2, dtype=F32))
    ret_tabs = _rope_tables(seq, RET_HEAD_DIM // 2, ret_inv)
    tab_spec = pl.BlockSpec((tm, LANES), lambda j, i: (i % rows_per_seq, 0))

    xf = x.reshape(batch * seq, d)
    memf = mem.reshape(batch * mem_tokens, d)
    seg = {"dq": 0, "dv": 2 * mix_w, "ca": 3 * mix_w, "cb": 4 * mix_w, "rq": 5 * mix_w, "rv": 7 * mix_w,
           "rg": 8 * mix_w, "gates": 9 * mix_w}

    def off(name, width=tn):
        return lambda j, o=seg[name] // width: o + j

    ident = lambda j: j

    xb, inv = _prep_norm(xf, norm_mix[0])

    for i in range(depth):
        lambda_init = 0.8 - 0.6 * math.exp(-0.3 * i)
        last = i == depth - 1

        dqk = _matmul(xb, w_in, i, [off("dq")], 2 * mix_w,
                      functools.partial(_epi_rope, half=DIFF_HEAD_DIM // 2), BF16, inv=inv,
                      aux=diff_tabs, aux_specs=(tab_spec, tab_spec), tm=tm, tn=tn)
        rqk = _matmul(xb, w_in, i, [off("rq")], 2 * mix_w,
                      functools.partial(_epi_rope, half=RET_HEAD_DIM // 2), BF16, inv=inv,
                      aux=ret_tabs, aux_specs=(tab_spec, tab_spec), tm=tm, tn=tn)
        dv = _matmul(xb, w_in, i, [off("dv")], mix_w, _epi_cast, BF16, inv=inv, tm=tm, tn=tn)
        rv = _matmul(xb, w_in, i, [off("rv")], mix_w, _epi_cast, BF16, inv=inv, tm=tm, tn=tn)
        u = _matmul(xb, w_in, i, [off("ca", tn_glu), off("cb", tn_glu)], mix_w, _epi_glu, F32, inv=inv,
                    tm=tm, tn=tn_glu)
        rgate = _matmul(xb, w_in, i, [off("rg")], mix_w, _epi_silu, F32, inv=inv, tm=tm, tn=tn)
        gates = _matmul(xb, w_in, i, [off("gates")], N_BRANCH * d, _epi_sigmoid, BF16, inv=inv,
                        tm=tm, tn=tn)

        ya = _diff_attention(dqk, dv, diff_lambda[i], diff_subln[i], batch, seq, lambda_init)
        yb = _conv_module(u, conv_w[i], conv_b[i], conv_ln_g[i], conv_ln_b[i], batch, seq)
        yr = _retention(rqk, rv, rgate, ret_gn_g[i], batch, seq)

        w_br = _cast_layer(w_branch.reshape(depth, N_BRANCH * mix_w, d), i).reshape(N_BRANCH, mix_w, d)
        xf, xb, inv = _merge_out(ya, yb, yr, w_br, gates, _cast_layer(w_out, i), xf, norm_xattn[i])

        hm = _rmsnorm(memf, norm_mem[i], BF16, tm=256)
        q = _matmul(xb, xattn_wq, i, [ident], d, _epi_cast, BF16, inv=inv, tm=tm, tn=tn)
        kv = _matmul(hm, xattn_wkv, i, [ident], 2 * d, _epi_cast, BF16,
                     tm=min(tm, batch * mem_tokens), tn=tn)
        o = _cross_attention(q, kv, batch, seq, mem_tokens)
        xf, xb, inv = _matmul_whole_rows(o, _cast_layer(xattn_wo, i), xf, norm_ffn[i], False,
                                         tm=TM_ROWS_D)

        act = _matmul(xb, ffn_w13, i, [ident, lambda j, o=ffn_hidden // tn_glu: o + j], ffn_hidden,
                      _epi_swiglu, BF16, inv=inv, tm=tm, tn=tn_glu)
        w2 = _cast_layer(ffn_w2, i)
        if last:
            (out,) = _matmul_whole_rows(act, w2, xf, norm_final, True, tm=TM_ROWS_WIDE)
        else:
            xf, xb, inv = _matmul_whole_rows(act, w2, xf, norm_mix[i + 1], False, tm=TM_ROWS_WIDE)

    return out.reshape(batch, seq, d)
```

```python
import functools
import math

import jax
import jax.numpy as jnp
from jax import lax
from jax.experimental import pallas as pl
from jax.experimental.pallas import tpu as pltpu

F32 = jnp.float32
BF16 = jnp.bfloat16

DIFF_HEADS = 8
DIFF_HEAD_DIM = 64
RET_HEADS = 8
RET_HEAD_DIM = 128
HEAD_W = 128
CONV_WIDTH = 31
N_BRANCH = 3
XATTN_HEADS = 4
ROPE_THETA = 10000.0
EPS = 1e-6
NEG_INF = -1e30

V7X_VMEM_BYTES = 64 * 1024 * 1024
VMEM_LIMIT_BYTES = V7X_VMEM_BYTES - 12 * 1024 * 1024
LANES = 128
SUBLANES = 8

TM = 1024
TN = 1024
TN_GATED = 512
TM_ROWS_WIDE = 256


def _params(semantics):
    return pltpu.CompilerParams(dimension_semantics=semantics, vmem_limit_bytes=VMEM_LIMIT_BYTES)


def _rmsnorm_kernel(x_ref, g_ref, o_ref):
    x = x_ref[...]
    y = x * lax.rsqrt(jnp.mean(x * x, axis=-1, keepdims=True) + EPS)
    o_ref[...] = (y * g_ref[...]).astype(o_ref.dtype)


def _rmsnorm(x, g, out_dtype, tm=512):
    m, d = x.shape
    return pl.pallas_call(
        _rmsnorm_kernel,
        out_shape=jax.ShapeDtypeStruct((m, d), out_dtype),
        grid=(m // tm,),
        in_specs=[pl.BlockSpec((tm, d), lambda i: (i, 0)),
                  pl.BlockSpec((1, d), lambda i: (0, 0))],
        out_specs=pl.BlockSpec((tm, d), lambda i: (i, 0)),
        compiler_params=_params(("parallel",)),
        name="rmsnorm",
    )(x, g.reshape(1, d))


def _inv_rms_lanes(sum_sq, d):
    return jnp.broadcast_to(lax.rsqrt(sum_sq * (1.0 / d) + EPS), (sum_sq.shape[0], LANES))


def _prep_norm_kernel(x_ref, g_ref, xb_ref, inv_ref):
    x = x_ref[...]
    xb_ref[...] = (x * g_ref[...]).astype(xb_ref.dtype)
    inv_ref[...] = _inv_rms_lanes(jnp.sum(x * x, axis=-1, keepdims=True), x.shape[1])


def _prep_norm(x, g, tm=512):
    m, d = x.shape
    return pl.pallas_call(
        _prep_norm_kernel,
        out_shape=(jax.ShapeDtypeStruct((m, d), BF16), jax.ShapeDtypeStruct((m, LANES), F32)),
        grid=(m // tm,),
        in_specs=[pl.BlockSpec((tm, d), lambda i: (i, 0)),
                  pl.BlockSpec((1, d), lambda i: (0, 0))],
        out_specs=(pl.BlockSpec((tm, d), lambda i: (i, 0)), pl.BlockSpec((tm, LANES), lambda i: (i, 0))),
        compiler_params=_params(("parallel",)),
        name="prep_norm",
    )(x, g.reshape(1, d))


def _cast_kernel(w_ref, o_ref):
    o_ref[...] = w_ref[0].astype(o_ref.dtype)


def _cast_layer(w, layer, rows=512):
    _, r, n = w.shape
    return pl.pallas_call(
        _cast_kernel,
        out_shape=jax.ShapeDtypeStruct((r, n), BF16),
        grid=(r // rows,),
        in_specs=[pl.BlockSpec((1, rows, n), lambda i: (layer, i, 0))],
        out_specs=pl.BlockSpec((rows, n), lambda i: (i, 0)),
        compiler_params=_params(("parallel",)),
        name="cast_weights",
    )(w)


def _epi_cast(accs, aux):
    return accs[0]


def _sigmoid(x):
    return 0.5 * jnp.tanh(0.5 * x) + 0.5


def _epi_sigmoid(accs, aux):
    return _sigmoid(accs[0])


def _epi_silu(accs, aux):
    return accs[0] * _sigmoid(accs[0])


def _epi_glu(accs, aux):
    return accs[0] * _sigmoid(accs[1])


def _epi_swiglu(accs, aux):
    return accs[0] * _sigmoid(accs[0]) * accs[1]


def _epi_rope(accs, aux, *, half):
    acc = accs[0]
    cos, sin_signed = aux[0][...], aux[1][...]
    tm, tn = acc.shape
    lane = lax.broadcasted_iota(jnp.int32, (tm, LANES), 1)
    first_half = (lane % (2 * half)) < half
    outs = []
    for c in range(tn // LANES):
        x = acc[:, c * LANES:(c + 1) * LANES]
        if 2 * half == LANES:
            partner = pltpu.roll(x, half, 1)
        else:
            partner = jnp.where(first_half, pltpu.roll(x, LANES - half, 1), pltpu.roll(x, half, 1))
        outs.append(x * cos + partner * sin_signed)
    return jnp.concatenate(outs, axis=1)


def _cast_weights_once(w_refs, wb_refs):
    @pl.when(pl.program_id(1) == 0)
    def _():
        for w, wb in zip(w_refs, wb_refs):
            wb[...] = w[0].astype(BF16)


def _mm_kernel(*refs, n_w, epilogue, row_scaled):
    a_ref = refs[0]
    w_refs = refs[1:1 + n_w]
    aux_refs = refs[1 + n_w:-1 - n_w]
    o_ref = refs[-1 - n_w]
    wb_refs = refs[-n_w:]
    _cast_weights_once(w_refs, wb_refs)
    a = a_ref[...]
    accs = [jnp.dot(a, wb[...], preferred_element_type=F32) for wb in wb_refs]
    if row_scaled:
        inv = jnp.tile(aux_refs[0][...], (1, o_ref.shape[1] // LANES))
        accs = [acc * inv for acc in accs]
        aux_refs = aux_refs[1:]
    o_ref[...] = epilogue(accs, aux_refs).astype(o_ref.dtype)


def _matmul(a, w, layer, col_blocks, n_out, epilogue, out_dtype, inv=None, aux=(), aux_specs=(),
            tm=1024, tn=1024):
    m, k = a.shape
    n_w = len(col_blocks)
    in_specs = [pl.BlockSpec((tm, k), lambda j, i: (i, 0))]
    for cb in col_blocks:
        in_specs.append(pl.BlockSpec((1, k, tn), lambda j, i, cb=cb: (layer, 0, cb(j))))
    if inv is not None:
        in_specs.append(pl.BlockSpec((tm, LANES), lambda j, i: (i, 0)))
        aux = (inv,) + tuple(aux)
    in_specs.extend(aux_specs)
    return pl.pallas_call(
        functools.partial(_mm_kernel, n_w=n_w, epilogue=epilogue, row_scaled=inv is not None),
        out_shape=jax.ShapeDtypeStruct((m, n_out), out_dtype),
        grid=(n_out // tn, m // tm),
        in_specs=in_specs,
        out_specs=pl.BlockSpec((tm, tn), lambda j, i: (i, j)),
        scratch_shapes=[pltpu.VMEM((k, tn), BF16) for _ in range(n_w)],
        compiler_params=_params(("arbitrary", "arbitrary")),
        name="matmul_" + getattr(epilogue, "__name__", "rope"),
    )(a, *([w] * n_w), *aux)


def _emit_rows(x, gain_ref, out_refs, final):
    d = x.shape[1]
    sum_sq = jnp.sum(x * x, axis=-1, keepdims=True)
    if final:
        out_refs[0][...] = x * lax.rsqrt(sum_sq * (1.0 / d) + EPS) * gain_ref[...]
    else:
        out_refs[0][...] = x
        out_refs[1][...] = (x * gain_ref[...]).astype(BF16)
        out_refs[2][...] = _inv_rms_lanes(sum_sq, d)


def _rows_out(m, d, tm, final):
    rows = lambda i: (i, 0)
    shapes = [jax.ShapeDtypeStruct((m, d), F32)]
    specs = [pl.BlockSpec((tm, d), rows)]
    if not final:
        shapes += [jax.ShapeDtypeStruct((m, d), BF16), jax.ShapeDtypeStruct((m, LANES), F32)]
        specs += [pl.BlockSpec((tm, d), rows), pl.BlockSpec((tm, LANES), rows)]
    return tuple(shapes), tuple(specs)


def _mm_whole_rows_kernel(a_ref, w_ref, r_ref, gain_ref, *out_refs, final):
    x = r_ref[...] + jnp.dot(a_ref[...], w_ref[...], preferred_element_type=F32)
    _emit_rows(x, gain_ref, out_refs, final)


def _matmul_whole_rows(a, w, res, gain, final, tm):
    m, k = a.shape
    d = w.shape[1]
    rows = lambda i: (i, 0)
    shapes, specs = _rows_out(m, d, tm, final)
    return pl.pallas_call(
        functools.partial(_mm_whole_rows_kernel, final=final),
        out_shape=shapes,
        grid=(m // tm,),
        in_specs=[pl.BlockSpec((tm, k), rows),
                  pl.BlockSpec((k, d), lambda i: (0, 0), pipeline_mode=pl.Buffered(1)),
                  pl.BlockSpec((tm, d), rows),
                  pl.BlockSpec((1, d), lambda i: (0, 0))],
        out_specs=specs,
        compiler_params=_params(("parallel",)),
        name="matmul_whole_rows",
    )(a, w, res, gain.reshape(1, d))


ONES_ROWS = 16


def _diff_attn_kernel(lam_ref, sub_ref, q_ref, k_ref, v_ref, o_ref, vt_ref, acc_ref, *, tq, group,
                      lambda_init):
    qi = pl.program_id(2)
    tk = tq
    nkv = v_ref.shape[0] // tk
    heads = [slice(g * HEAD_W, (g + 1) * HEAD_W) for g in range(group)]

    @pl.when(qi == 0)
    def _():
        for g in range(group):
            for jb in range(nkv):
                vt_ref[g, jb, 0:HEAD_W, :] = v_ref[jb * tk:(jb + 1) * tk, heads[g]].astype(F32).T.astype(BF16)
                row = lax.broadcasted_iota(jnp.int32, (ONES_ROWS, tk), 0)
                vt_ref[g, jb, HEAD_W:HEAD_W + ONES_ROWS, :] = jnp.where(row == 0, 1.0, 0.0).astype(BF16)

    scale = DIFF_HEAD_DIM ** -0.5 * math.log2(math.e)
    dim = lax.broadcasted_iota(jnp.int32, (HEAD_W, tq), 0)
    qsts = []
    for g in range(group):
        qt = (q_ref[:, heads[g]].astype(F32) * scale).T
        qsts.append(jnp.concatenate([jnp.where(dim < DIFF_HEAD_DIM, qt, 0.0),
                                     jnp.where(dim >= DIFF_HEAD_DIM, qt, 0.0)], axis=1).astype(BF16))

    def block(j, carry, masked):
        new = []
        logits = [jnp.dot(k_ref[pl.ds(pl.multiple_of(j * tk, tk), tk), heads[g]], qsts[g],
                          preferred_element_type=F32) for g in range(group)]
        for g in range(group):
            m_prev = carry[g]
            s = logits[g]
            if masked:
                key = lax.broadcasted_iota(jnp.int32, (tk, 2 * tq), 0)
                qry = lax.broadcasted_iota(jnp.int32, (tk, 2 * tq), 1)
                qry = jnp.where(qry >= tq, qry - tq, qry)
                s = jnp.where(key <= qry, s, NEG_INF)
            sb = s.astype(BF16)
            m_new = jnp.maximum(m_prev, jnp.max(sb, axis=0, keepdims=True))
            alpha = jnp.exp2(m_prev.astype(F32) - m_new.astype(F32))
            p = jnp.exp2(sb - m_new)
            acc_ref[g] = alpha * acc_ref[g] + jnp.dot(vt_ref[g, j], p, preferred_element_type=F32)
            new.append(m_new)
        return tuple(new)

    acc_ref[...] = jnp.zeros_like(acc_ref)
    init = tuple(jnp.full((1, 2 * tq), NEG_INF, BF16) for _ in range(group))
    carry = lax.fori_loop(0, qi, functools.partial(block, masked=False), init)
    block(qi, carry, masked=True)

    lam_p = lam_ref[...]
    lam = (jnp.exp(jnp.sum(lam_p[0:1] * lam_p[1:2], axis=-1, keepdims=True))
           - jnp.exp(jnp.sum(lam_p[2:3] * lam_p[3:4], axis=-1, keepdims=True)) + lambda_init)
    for g in range(group):
        o = acc_ref[g, 0:HEAD_W, :] * (1.0 / acc_ref[g, HEAD_W:HEAD_W + 1, :])
        y = o[:, :tq] - lam * o[:, tq:]
        y = y * lax.rsqrt(jnp.mean(y * y, axis=0, keepdims=True) + EPS) * sub_ref[...]
        o_ref[:, heads[g]] = (y * (1.0 - lambda_init)).T.astype(o_ref.dtype)


def _diff_attention(qk, v, diff_lambda, diff_subln, batch, seq, lambda_init, tq=256, group=8):
    nq = seq // tq
    n_groups = DIFF_HEADS // group
    gw = group * HEAD_W
    return pl.pallas_call(
        functools.partial(_diff_attn_kernel, tq=tq, group=group, lambda_init=lambda_init),
        out_shape=jax.ShapeDtypeStruct((batch * seq, DIFF_HEADS * HEAD_W), BF16),
        grid=(batch, n_groups, nq),
        in_specs=[pl.BlockSpec((4, DIFF_HEAD_DIM), lambda b, h, i: (0, 0)),
                  pl.BlockSpec((HEAD_W, 1), lambda b, h, i: (0, 0)),
                  pl.BlockSpec((tq, gw), lambda b, h, i: (b * nq + i, h)),
                  pl.BlockSpec((seq, gw), lambda b, h, i: (b, n_groups + h)),
                  pl.BlockSpec((seq, gw), lambda b, h, i: (b, h))],
        out_specs=pl.BlockSpec((tq, gw), lambda b, h, i: (b * nq + i, h)),
        scratch_shapes=[pltpu.VMEM((group, nq, HEAD_W + ONES_ROWS, tq), BF16),
                        pltpu.VMEM((group, HEAD_W + ONES_ROWS, 2 * tq), F32)],
        compiler_params=_params(("parallel", "parallel", "arbitrary")),
        name="diff_attention",
    )(diff_lambda, diff_subln.reshape(HEAD_W, 1), qk, qk, v)


CONV_HALO = 32
LN_ROWS = 64


def _conv_kernel(w_ref, b_ref, g_ref, beta_ref, prev_ref, cur_ref, o_ref, sh_ref, y_ref, *, tt, rows,
                 lanes):
    i = pl.program_id(1)
    ch = sh_ref.shape[2]
    n_ext = CONV_HALO + tt

    @pl.when(i == 0)
    def _():
        sh_ref[0, 0:CONV_HALO, :] = jnp.zeros((CONV_HALO, ch), F32)

    @pl.when(i > 0)
    def _():
        sh_ref[0, 0:CONV_HALO, :] = prev_ref[...]

    sh_ref[0, CONV_HALO:n_ext, :] = cur_ref[...]
    for c0 in range(0, ch, LANES):
        ext = sh_ref[0, :, c0:c0 + LANES]
        for s in range(1, SUBLANES):
            sh_ref[s, 0:n_ext - SUBLANES, c0:c0 + LANES] = pltpu.roll(ext, n_ext - s, 0)[0:n_ext - SUBLANES]

    first = CONV_HALO - (CONV_WIDTH - 1)
    groups = rows // SUBLANES

    def row_chunk(c, carry):
        r0 = pl.multiple_of(c * rows, rows)
        for c0 in range(0, ch, lanes):
            acc = jnp.zeros((groups, SUBLANES, lanes), F32)
            for s in range(SUBLANES):
                offs = [o for o in range(first, first + CONV_WIDTH) if o % SUBLANES == s]
                a_lo, a_hi = offs[0] // SUBLANES, offs[-1] // SUBLANES
                span = sh_ref[s, pl.ds(r0 + SUBLANES * a_lo, rows + SUBLANES * (a_hi - a_lo)), c0:c0 + lanes]
                span = span.reshape(groups + a_hi - a_lo, SUBLANES, lanes)
                for o in offs:
                    a = o // SUBLANES - a_lo
                    acc = acc + w_ref[o - first, :, c0:c0 + lanes] * span[a:a + groups]
            y_ref[pl.ds(r0, rows), c0:c0 + lanes] = acc.reshape(rows, lanes)
        return carry

    lax.fori_loop(0, tt // rows, row_chunk, 0)

    for r0 in range(0, tt, LN_ROWS):
        y = y_ref[r0:r0 + LN_ROWS, :] + b_ref[...]
        mu = jnp.mean(y, axis=-1, keepdims=True)
        var = jnp.mean(jnp.square(y - mu), axis=-1, keepdims=True)
        y = (y - mu) * lax.rsqrt(var + EPS) * g_ref[...] + beta_ref[...]
        o_ref[r0:r0 + LN_ROWS, :] = (y * _sigmoid(y)).astype(o_ref.dtype)


def _conv_module(u, conv_w, conv_b, ln_g, ln_b, batch, seq, tt=256, rows=128, lanes=128):
    ch = u.shape[1]
    conv_w = jnp.broadcast_to(conv_w[:, None, :], (CONV_WIDTH, SUBLANES, ch))
    nt = seq // tt
    halo_per_tile = tt // CONV_HALO
    vec = lambda a: a.reshape(1, ch)
    row_spec = pl.BlockSpec((1, ch), lambda b, i: (0, 0))
    return pl.pallas_call(
        functools.partial(_conv_kernel, tt=tt, rows=rows, lanes=lanes),
        out_shape=jax.ShapeDtypeStruct((batch * seq, ch), BF16),
        grid=(batch, nt),
        in_specs=[pl.BlockSpec((CONV_WIDTH, SUBLANES, ch), lambda b, i: (0, 0, 0)),
                  row_spec, row_spec, row_spec,
                  pl.BlockSpec((CONV_HALO, ch),
                               lambda b, i: (jnp.maximum((b * nt + i) * halo_per_tile - 1, 0), 0)),
                  pl.BlockSpec((tt, ch), lambda b, i: (b * nt + i, 0))],
        out_specs=pl.BlockSpec((tt, ch), lambda b, i: (b * nt + i, 0)),
        scratch_shapes=[pltpu.VMEM((SUBLANES, CONV_HALO + tt, ch), F32), pltpu.VMEM((tt, ch), F32)],
        compiler_params=_params(("parallel", "arbitrary")),
        name="conv_module",
    )(conv_w, vec(conv_b), vec(ln_g), vec(ln_b), u, u)


def _retention_kernel(lg_ref, g_ref, q_ref, k_ref, v_ref, gate_ref, o_ref, state_ref, *, chunk, seq, group):
    scale = RET_HEAD_DIM ** -0.5
    row = lax.broadcasted_iota(jnp.int32, (chunk, chunk), 0)
    col = lax.broadcasted_iota(jnp.int32, (chunk, chunk), 1)
    rel = (row - col).astype(F32)
    idx = lax.broadcasted_iota(jnp.int32, (chunk, 1), 0).astype(F32)
    heads = [slice(g * HEAD_W, (g + 1) * HEAD_W) for g in range(group)]
    decays = []
    for g in range(group):
        lg = lg_ref[g][:, 0:1]
        decays.append((jnp.where(rel >= 0, jnp.exp(lg * jnp.maximum(rel, 0.0)), 0.0) * scale,
                       jnp.exp(lg * (idx + 1.0)) * scale,
                       jnp.exp(lg * (chunk - 1.0 - idx)),
                       jnp.exp(lg * chunk)))
    state_ref[...] = jnp.zeros_like(state_ref)
    for n in range(seq // chunk):
        sl = slice(n * chunk, (n + 1) * chunk)
        for g in range(group):
            decay, q_decay, k_decay, chunk_decay = decays[g]
            q = q_ref[sl, heads[g]]
            k = k_ref[sl, heads[g]]
            v = v_ref[sl, heads[g]]
            scores = lax.dot_general(q, k, (((1,), (1,)), ((), ())), preferred_element_type=F32) * decay
            y = jnp.dot(scores.astype(BF16), v, preferred_element_type=F32)
            state = state_ref[g]
            y = y + jnp.dot((q.astype(F32) * q_decay).astype(BF16), state.astype(BF16),
                            preferred_element_type=F32)
            kd_t = (k.astype(F32) * k_decay).T.astype(BF16)
            state_ref[g] = state * chunk_decay + jnp.dot(kd_t, v, preferred_element_type=F32)
            mu = jnp.mean(y, axis=-1, keepdims=True)
            var = jnp.mean(jnp.square(y - mu), axis=-1, keepdims=True)
            y = (y - mu) * lax.rsqrt(var + EPS) * g_ref[:, heads[g]]
            o_ref[sl, heads[g]] = (gate_ref[sl, heads[g]] * y).astype(o_ref.dtype)


def _retention(qk, v, gate, gn_g, batch, seq, chunk=256, group=2):
    log_gamma = jnp.log1p(-jnp.exp2(-5.0 - jnp.arange(RET_HEADS, dtype=F32)))
    lg = jnp.broadcast_to(log_gamma[:, None, None], (RET_HEADS, 1, LANES))
    n_groups = RET_HEADS // group
    gw = group * HEAD_W
    head = lambda b, h: (b, h)
    return pl.pallas_call(
        functools.partial(_retention_kernel, chunk=chunk, seq=seq, group=group),
        out_shape=jax.ShapeDtypeStruct((batch * seq, RET_HEADS * HEAD_W), BF16),
        grid=(batch, n_groups),
        in_specs=[pl.BlockSpec((group, 1, LANES), lambda b, h: (h, 0, 0)),
                  pl.BlockSpec((1, gw), lambda b, h: (0, h)),
                  pl.BlockSpec((seq, gw), head),
                  pl.BlockSpec((seq, gw), lambda b, h: (b, n_groups + h)),
                  pl.BlockSpec((seq, gw), head),
                  pl.BlockSpec((seq, gw), head)],
        out_specs=pl.BlockSpec((seq, gw), head),
        scratch_shapes=[pltpu.VMEM((group, RET_HEAD_DIM, RET_HEAD_DIM), F32)],
        compiler_params=_params(("parallel", "parallel")),
        name="retention",
    )(lg, gn_g.reshape(1, -1), qk, qk, v, gate)


def _merge_out_kernel(ya_ref, yb_ref, yr_ref, wbr_ref, g_ref, wo_ref, r_ref, gain_ref,
                      x_ref, xb_ref, inv_ref):
    d = x_ref.shape[1]
    merged = None
    for b, y_ref in enumerate((ya_ref, yb_ref, yr_ref)):
        term = g_ref[:, b * d:(b + 1) * d].astype(F32) * jnp.dot(y_ref[...], wbr_ref[b],
                                                                preferred_element_type=F32)
        merged = term if merged is None else merged + term
    x = r_ref[...] + jnp.dot(merged.astype(BF16), wo_ref[...], preferred_element_type=F32)
    _emit_rows(x, gain_ref, (x_ref, xb_ref, inv_ref), final=False)


def _merge_out(ya, yb, yr, w_branch, gates, w_out, res, g_next, tm=TM_ROWS_WIDE):
    m, kdim = ya.shape
    d = res.shape[1]
    rows = lambda i: (i, 0)
    once = pl.Buffered(1)
    return pl.pallas_call(
        _merge_out_kernel,
        out_shape=(jax.ShapeDtypeStruct((m, d), F32), jax.ShapeDtypeStruct((m, d), BF16),
                   jax.ShapeDtypeStruct((m, LANES), F32)),
        grid=(m // tm,),
        in_specs=[pl.BlockSpec((tm, kdim), rows), pl.BlockSpec((tm, kdim), rows),
                  pl.BlockSpec((tm, kdim), rows),
                  pl.BlockSpec((N_BRANCH, kdim, d), lambda i: (0, 0, 0), pipeline_mode=once),
                  pl.BlockSpec((tm, N_BRANCH * d), rows),
                  pl.BlockSpec((d, d), lambda i: (0, 0), pipeline_mode=once),
                  pl.BlockSpec((tm, d), rows),
                  pl.BlockSpec((1, d), lambda i: (0, 0))],
        out_specs=(pl.BlockSpec((tm, d), rows), pl.BlockSpec((tm, d), rows),
                   pl.BlockSpec((tm, LANES), rows)),
        compiler_params=_params(("parallel",)),
        name="merge_out",
    )(ya, yb, yr, w_branch, gates, w_out, res, g_next.reshape(1, d))


def _xattn_block_kernel(xb_ref, inv_ref, wq_ref, k_ref, v_ref, wo_ref, r_ref, gain_ref,
                        x_ref, xbo_ref, invo_ref, *, scale, hd):
    d = x_ref.shape[1]
    q = jnp.dot(xb_ref[...], wq_ref[...], preferred_element_type=F32)
    q = (q * jnp.tile(inv_ref[...], (1, d // LANES))).astype(BF16)
    heads = [slice(h * hd, (h + 1) * hd) for h in range(XATTN_HEADS)]
    logits = [lax.dot_general(q[:, hs], k_ref[:, hs], (((1,), (1,)), ((), ())),
                              preferred_element_type=F32) for hs in heads]
    outs = []
    for hs, s in zip(heads, logits):
        s = s * scale
        p = jnp.exp(s - jnp.max(s, axis=-1, keepdims=True))
        l = jnp.sum(p, axis=-1, keepdims=True)
        outs.append((jnp.dot(p.astype(BF16), v_ref[:, hs], preferred_element_type=F32) / l).astype(BF16))
    x = r_ref[...] + jnp.dot(jnp.concatenate(outs, axis=1), wo_ref[...], preferred_element_type=F32)
    _emit_rows(x, gain_ref, (x_ref, xbo_ref, invo_ref), final=False)


def _cross_attention_block(xb, inv, wq, kv, wo, res, gain, batch, seq, mem_tokens, tm=TM_ROWS_WIDE):
    d = res.shape[1]
    hd = d // XATTN_HEADS
    nq = seq // tm
    rows = lambda b, i: (b * nq + i, 0)
    once = pl.Buffered(1)
    shapes, _ = _rows_out(batch * seq, d, tm, False)
    return pl.pallas_call(
        functools.partial(_xattn_block_kernel, scale=hd ** -0.5, hd=hd),
        out_shape=shapes,
        grid=(batch, nq),
        in_specs=[pl.BlockSpec((tm, d), rows),
                  pl.BlockSpec((tm, LANES), rows),
                  pl.BlockSpec((d, d), lambda b, i: (0, 0), pipeline_mode=once),
                  pl.BlockSpec((mem_tokens, d), lambda b, i: (b, 0)),
                  pl.BlockSpec((mem_tokens, d), lambda b, i: (b, 1)),
                  pl.BlockSpec((d, d), lambda b, i: (0, 0), pipeline_mode=once),
                  pl.BlockSpec((tm, d), rows),
                  pl.BlockSpec((1, d), lambda b, i: (0, 0))],
        out_specs=(pl.BlockSpec((tm, d), rows), pl.BlockSpec((tm, d), rows),
                   pl.BlockSpec((tm, LANES), rows)),
        compiler_params=_params(("parallel", "arbitrary")),
        name="cross_attention_block",
    )(xb, inv, wq, kv, kv, wo, res, gain.reshape(1, d))


def _rope_tables(seq, half, inv_freq):
    ang = jnp.arange(seq, dtype=F32)[:, None] * inv_freq[None, :]
    cos, sin = jnp.cos(ang), jnp.sin(ang)
    reps = LANES // (2 * half)
    cos_l = jnp.tile(jnp.concatenate([cos, cos], axis=1), (1, reps))
    sin_l = jnp.tile(jnp.concatenate([-sin, sin], axis=1), (1, reps))
    return cos_l, sin_l


def kernel(x, mem, norm_mix, w_in, diff_lambda, diff_subln, conv_w, conv_b, conv_ln_g, conv_ln_b,
           ret_gn_g, w_branch, w_out, norm_xattn, norm_mem, xattn_wq, xattn_wkv, xattn_wo,
           norm_ffn, ffn_w13, ffn_w2, norm_final):
    batch, seq, d = x.shape
    mem_tokens = mem.shape[1]
    depth = w_in.shape[0]
    mix_w = d // 2
    ffn_hidden = ffn_w2.shape[1]
    tm, tn, tn_glu = TM, TN, TN_GATED
    rows_per_seq = seq // tm

    diff_inv = ROPE_THETA ** (-jnp.arange(0, DIFF_HEAD_DIM, 2, dtype=F32) / DIFF_HEAD_DIM)
    diff_tabs = _rope_tables(seq, DIFF_HEAD_DIM // 2, diff_inv)
    ret_inv = 1.0 / (ROPE_THETA ** jnp.linspace(0.0, 1.0, RET_HEAD_DIM // 2, dtype=F32))
    ret_tabs = _rope_tables(seq, RET_HEAD_DIM // 2, ret_inv)
    tab_spec = pl.BlockSpec((tm, LANES), lambda j, i: (i % rows_per_seq, 0))

    xf = x.reshape(batch * seq, d)
    memf = mem.reshape(batch * mem_tokens, d)
    seg = {"dq": 0, "dv": 2 * mix_w, "ca": 3 * mix_w, "cb": 4 * mix_w, "rq": 5 * mix_w, "rv": 7 * mix_w,
           "rg": 8 * mix_w, "gates": 9 * mix_w}

    def off(name, width=tn):
        return lambda j, o=seg[name] // width: o + j

    ident = lambda j: j

    xb, inv = _prep_norm(xf, norm_mix[0])

    for i in range(depth):
        lambda_init = 0.8 - 0.6 * math.exp(-0.3 * i)
        last = i == depth - 1

        dqk = _matmul(xb, w_in, i, [off("dq")], 2 * mix_w,
                      functools.partial(_epi_rope, half=DIFF_HEAD_DIM // 2), BF16, inv=inv,
                      aux=diff_tabs, aux_specs=(tab_spec, tab_spec), tm=tm, tn=tn)
        rqk = _matmul(xb, w_in, i, [off("rq")], 2 * mix_w,
                      functools.partial(_epi_rope, half=RET_HEAD_DIM // 2), BF16, inv=inv,
                      aux=ret_tabs, aux_specs=(tab_spec, tab_spec), tm=tm, tn=tn)
        dv = _matmul(xb, w_in, i, [off("dv")], mix_w, _epi_cast, BF16, inv=inv, tm=tm, tn=tn)
        rv = _matmul(xb, w_in, i, [off("rv")], mix_w, _epi_cast, BF16, inv=inv, tm=tm, tn=tn)
        u = _matmul(xb, w_in, i, [off("ca", tn_glu), off("cb", tn_glu)], mix_w, _epi_glu, F32, inv=inv,
                    tm=tm, tn=tn_glu)
        rgate = _matmul(xb, w_in, i, [off("rg")], mix_w, _epi_silu, F32, inv=inv, tm=tm, tn=tn)
        gates = _matmul(xb, w_in, i, [off("gates")], N_BRANCH * d, _epi_sigmoid, BF16, inv=inv,
                        tm=tm, tn=tn)

        ya = _diff_attention(dqk, dv, diff_lambda[i], diff_subln[i], batch, seq, lambda_init)
        yb = _conv_module(u, conv_w[i], conv_b[i], conv_ln_g[i], conv_ln_b[i], batch, seq)
        yr = _retention(rqk, rv, rgate, ret_gn_g[i], batch, seq)

        w_br = _cast_layer(w_branch.reshape(depth, N_BRANCH * mix_w, d), i).reshape(N_BRANCH, mix_w, d)
        xf, xb, inv = _merge_out(ya, yb, yr, w_br, gates, _cast_layer(w_out, i), xf, norm_xattn[i])

        hm = _rmsnorm(memf, norm_mem[i], BF16, tm=256)
        kv = _matmul(hm, xattn_wkv, i, [ident], 2 * d, _epi_cast, BF16,
                     tm=min(tm, batch * mem_tokens), tn=tn)
        xf, xb, inv = _cross_attention_block(xb, inv, _cast_layer(xattn_wq, i), kv, _cast_layer(xattn_wo, i),
                                             xf, norm_ffn[i], batch, seq, mem_tokens)

        act = _matmul(xb, ffn_w13, i, [ident, lambda j, o=ffn_hidden // tn_glu: o + j], ffn_hidden,
                      _epi_swiglu, BF16, inv=inv, tm=tm, tn=tn_glu)
        w2 = _cast_layer(ffn_w2, i)
        if last:
            (out,) = _matmul_whole_rows(act, w2, xf, norm_final, True, tm=TM_ROWS_WIDE)
        else:
            xf, xb, inv = _matmul_whole_rows(act, w2, xf, norm_mix[i + 1], False, tm=TM_ROWS_WIDE)

    return out.reshape(batch, seq, d)
```
